```python
import jax, jax.numpy as jnp
from jax import lax
import numpy as np

D_MODEL = 1024
BATCH = 8
SEQ = 4096
DEPTH = 4
DEC_BATCH = 1
DEC_SEQ = 16384
PAST_LEN = 128

CONV_DIM = D_MODEL
CONV_WIDTH = 31
CONV_PAD = CONV_WIDTH // 2
HG_DIM = D_MODEL
HG_HEAD_DIM = 128
HG_HEADS = HG_DIM // HG_HEAD_DIM
CHUNK = 64
SUB = 16
EXP_CLAMP = 60.0
N_EXPERTS = 16
N_GROUPS = 4
EXPERTS_PER_GROUP = N_EXPERTS // N_GROUPS
TOP_K = 2
D_EXPERT = 2 * D_MODEL
MOE_BLOCK = 128
N_PROJ = 2 * CONV_DIM + 5 * HG_DIM + 2 * D_MODEL
SPLITS = (CONV_DIM, 2 * CONV_DIM, 2 * CONV_DIM + HG_DIM, 2 * CONV_DIM + 2 * HG_DIM,
          2 * CONV_DIM + 3 * HG_DIM, 2 * CONV_DIM + 4 * HG_DIM, 2 * CONV_DIM + 5 * HG_DIM,
          2 * CONV_DIM + 5 * HG_DIM + D_MODEL)
ALPHA = (2 * DEPTH) ** 0.25
BETA = (8 * DEPTH) ** -0.25
LN_EPS = 1e-5
RMS_EPS = 1e-6

kernel_name = "hybrid_conv_hgrn2_grouped_moe_encoder"


def _layer_norm(x, g, b):
    xf = x.astype(jnp.float32)
    mu = jnp.mean(xf, axis=-1, keepdims=True)
    var = jnp.mean(jnp.square(xf - mu), axis=-1, keepdims=True)
    return ((xf - mu) * lax.rsqrt(var + LN_EPS) * g + b).astype(x.dtype)


def _gla_chunked(q, k, v, log_f):
    bsz, seqlen, nh, dk = q.shape
    dv = v.shape[-1]
    nc, ns = seqlen // CHUNK, CHUNK // SUB

    def to_chunks(t):
        return t.reshape(bsz, nc, ns, SUB, nh, t.shape[-1]).transpose(1, 0, 4, 2, 3, 5)

    causal = jnp.tril(jnp.ones((SUB, SUB), dtype=bool))[:, :, None]
    earlier = (jnp.arange(ns)[:, None] > jnp.arange(ns)[None, :])[:, :, None, None]
    causal_f = causal.astype(jnp.float32)
    earlier_f = earlier.astype(jnp.float32)

    def step(state, inp):
        qc, kc, vc, gc = inp
        b = jnp.cumsum(gc.reshape(bsz, nh, CHUNK, dk), axis=2).reshape(bsz, nh, ns, SUB, dk)
        b_last = b[:, :, -1, -1]
        ref = jnp.concatenate([jnp.zeros_like(b[:, :, :1, -1]), b[:, :, :-1, -1]], axis=2)
        out = jnp.einsum("bhntk,bhkv->bhntv", qc * jnp.exp(b), state)
        diff = jnp.where(causal, b[:, :, :, :, None, :] - b[:, :, :, None, :, :], 0.0)
        a_diag = jnp.einsum("bhntk,bhnsk,bhntsk->bhnts", qc, kc, jnp.exp(diff) * causal_f)
        out = out + jnp.einsum("bhnts,bhnsv->bhntv", a_diag, vc)
        q_ref = qc * jnp.exp(b - ref[:, :, :, None, :])
        e_off = jnp.where(earlier, ref[:, :, :, None, None, :] - b[:, :, None, :, :, :], 0.0)
        k_ref = kc[:, :, None] * (jnp.exp(e_off) * earlier_f)
        a_off = jnp.einsum("bhitk,bhijsk->bhijts", q_ref, k_ref)
        out = out + jnp.einsum("bhijts,bhjsv->bhitv", a_off, vc)
        k_dec = kc * jnp.exp(b_last[:, :, None, None, :] - b)
        state = jnp.exp(b_last)[..., None] * state + jnp.einsum("bhntk,bhntv->bhkv", k_dec, vc)
        return state, out

    state0 = jnp.zeros((bsz, nh, dk, dv), jnp.float32)
    _, oc = lax.scan(step, state0, (to_chunks(q), to_chunks(k), to_chunks(v), to_chunks(log_f)))
    return oc.transpose(1, 0, 3, 4, 2, 5).reshape(bsz, seqlen, nh, dv)


def _mixer(h, w_in, dw_w, dw_b, conv_ln_g, conv_ln_b, w_conv_out, lb, hg_norm_g, w_hg_out, w_o):
    bsz, seqlen, _ = h.shape
    proj = h @ w_in
    u_val, u_gate, q, f_fw, f_bw, v, og, gate_a, gate_b = jnp.split(proj, SPLITS, axis=-1)

    u = u_val * jax.nn.sigmoid(u_gate)
    u = lax.conv_general_dilated(u, dw_w[:, None, :], window_strides=(1,),
                                 padding=[(CONV_PAD, CONV_PAD)],
                                 dimension_numbers=("NWC", "WIO", "NWC"),
                                 feature_group_count=CONV_DIM) + dw_b
    u = jax.nn.silu(_layer_norm(u, conv_ln_g, conv_ln_b))
    branch_a = u @ w_conv_out

    def heads(t):
        return t.astype(jnp.float32).reshape(bsz, seqlen, HG_HEADS, HG_HEAD_DIM)

    qh = jax.nn.silu(heads(q)) * HG_HEAD_DIM ** -0.5
    vh = heads(v)
    lb_h = lb.reshape(2, HG_HEADS, HG_HEAD_DIM)

    def forget(f_logit, lb_dir):
        log_f = jax.nn.log_sigmoid(f_logit) + jnp.log1p(
            lb_dir * jnp.exp(jnp.minimum(-f_logit, EXP_CLAMP)))
        log_f = jnp.minimum(log_f, 0.0)
        return log_f, -jnp.expm1(log_f)

    logf_fw, k_fw = forget(heads(f_fw), lb_h[0])
    logf_bw, k_bw = forget(heads(f_bw), lb_h[1])
    o_fw = _gla_chunked(qh, k_fw, vh, logf_fw)
    rev = lambda t: jnp.flip(t, axis=1)
    o_bw = rev(_gla_chunked(rev(qh), rev(k_bw), rev(vh), rev(logf_bw)))
    o = o_fw + o_bw
    o = o * lax.rsqrt(jnp.mean(jnp.square(o), axis=-1, keepdims=True) + RMS_EPS) \
        * hg_norm_g.astype(jnp.float32).reshape(HG_HEADS, HG_HEAD_DIM)
    o = (o.reshape(bsz, seqlen, HG_DIM) * jax.nn.silu(og.astype(jnp.float32))).astype(h.dtype)
    branch_b = o @ w_hg_out

    merged = jax.nn.sigmoid(gate_a) * branch_a + jax.nn.sigmoid(gate_b) * branch_b
    return merged @ w_o


def _moe(h, w_router, b_router, w_e1, w_e2):
    bsz, seqlen, d = h.shape
    x2d = h.reshape(-1, d)
    n = x2d.shape[0]
    logits = (x2d @ w_router).astype(jnp.float32) + b_router.astype(jnp.float32)
    scores = jax.nn.softmax(logits, axis=-1).reshape(n, N_GROUPS, EXPERTS_PER_GROUP)
    group_score = jnp.sum(lax.top_k(scores, TOP_K)[0], axis=-1)
    g_sel = jnp.argmax(group_score, axis=-1)
    in_group = jnp.take_along_axis(scores, g_sel[:, None, None], axis=1)[:, 0]
    top_w, top_i = lax.top_k(in_group, TOP_K)
    gate = top_w / jnp.sum(top_w, axis=-1, keepdims=True)
    expert = g_sel[:, None].astype(jnp.int32) * EXPERTS_PER_GROUP + top_i.astype(jnp.int32)

    n_assign = n * TOP_K
    e_flat = expert.reshape(-1)
    order = jnp.argsort(e_flat)
    e_sorted = e_flat[order]
    counts = jnp.bincount(e_flat, length=N_EXPERTS)
    starts = jnp.cumsum(counts) - counts
    padded = (counts + MOE_BLOCK - 1) // MOE_BLOCK * MOE_BLOCK
    pad_end = jnp.cumsum(padded)
    dest = pad_end[e_sorted] - padded[e_sorted] + jnp.arange(n_assign, dtype=jnp.int32) - starts[e_sorted]
    n_blocks = -(-n_assign // MOE_BLOCK) + N_EXPERTS
    tok_flat = jnp.arange(n_assign, dtype=jnp.int32) // TOP_K
    tok_buf = jnp.zeros((n_blocks * MOE_BLOCK,), jnp.int32).at[dest].set(tok_flat[order])
    w_buf = jnp.zeros((n_blocks * MOE_BLOCK,), jnp.float32).at[dest].set(gate.reshape(-1)[order])
    block_expert = jnp.minimum(
        jnp.searchsorted(pad_end, jnp.arange(n_blocks, dtype=jnp.int32) * MOE_BLOCK, side="right"),
        N_EXPERTS - 1)
    x_blocks = x2d[tok_buf].reshape(n_blocks, MOE_BLOCK, d)

    def run_block(args):
        xb, e = args
        return jax.nn.gelu(xb @ w_e1[e]) @ w_e2[e]

    y_blocks = lax.map(run_block, (x_blocks, block_expert))
    y = jnp.zeros_like(x2d).at[tok_buf].add(
        (y_blocks.reshape(-1, d) * w_buf[:, None]).astype(h.dtype))
    return y.reshape(bsz, seqlen, d)


def _trunk(x, ln_in_g, ln_in_b, w_in, dw_w, dw_b, conv_ln_g, conv_ln_b, w_conv_out,
           hg_lower, hg_norm_g, w_hg_out, w_o, ln1_g, ln1_b, w_router, b_router,
           w_e1, w_e2, ln2_g, ln2_b):
    h = _layer_norm(x, ln_in_g, ln_in_b)
    lb_p = jax.nn.softmax(hg_lower.astype(jnp.float32), axis=0)
    lb_all = jnp.clip(jnp.cumsum(lb_p, axis=0) - lb_p[0:1], 0.0, 1.0)
    for l in range(DEPTH):
        mix = _mixer(h, w_in[l], dw_w[l], dw_b[l], conv_ln_g[l], conv_ln_b[l], w_conv_out[l],
                     lb_all[l], hg_norm_g[l], w_hg_out[l], w_o[l])
        h = _layer_norm(ALPHA * h + mix, ln1_g[l], ln1_b[l])
        ffn = _moe(h, w_router, b_router, w_e1[l], w_e2[l])
        h = _layer_norm(ALPHA * h + ffn, ln2_g[l], ln2_b[l])
    return h


def setup_inputs(seed: int = 0) -> dict:
    key = jax.random.key(seed)
    ks = jax.random.split(key, 24)

    def nrm(k, shape, scale):
        return jax.random.normal(k, shape, jnp.float32) * scale

    return {
        "x_prompt": nrm(ks[0], (BATCH, SEQ, D_MODEL), 1.0),
        "x_sample": nrm(ks[1], (DEC_BATCH, DEC_SEQ, D_MODEL), 1.0),
        "ln_in_g": 1.0 + nrm(ks[2], (D_MODEL,), 0.02),
        "ln_in_b": nrm(ks[3], (D_MODEL,), 0.02),
        "w_in": nrm(ks[4], (DEPTH, D_MODEL, N_PROJ), D_MODEL ** -0.5),
        "dw_w": nrm(ks[5], (DEPTH, CONV_WIDTH, CONV_DIM), CONV_WIDTH ** -0.5),
        "dw_b": nrm(ks[6], (DEPTH, CONV_DIM), 0.02),
        "conv_ln_g": 1.0 + nrm(ks[7], (DEPTH, CONV_DIM), 0.02),
        "conv_ln_b": nrm(ks[8], (DEPTH, CONV_DIM), 0.02),
        "w_conv_out": nrm(ks[9], (DEPTH, CONV_DIM, D_MODEL), CONV_DIM ** -0.5),
        "hg_lower": 1.0 + nrm(ks[10], (DEPTH, 2, HG_DIM), 0.1),
        "hg_norm_g": 1.0 + nrm(ks[11], (DEPTH, HG_DIM), 0.02),
        "w_hg_out": nrm(ks[12], (DEPTH, HG_DIM, D_MODEL), HG_DIM ** -0.5),
        "w_o": nrm(ks[13], (DEPTH, D_MODEL, D_MODEL), D_MODEL ** -0.5 * BETA),
        "ln1_g": 1.0 + nrm(ks[14], (DEPTH, D_MODEL), 0.02),
        "ln1_b": nrm(ks[15], (DEPTH, D_MODEL), 0.02),
        "w_router": nrm(ks[16], (D_MODEL, N_EXPERTS), D_MODEL ** -0.5),
        "b_router": nrm(ks[17], (N_EXPERTS,), 0.01),
        "w_e1": nrm(ks[18], (DEPTH, N_EXPERTS, D_MODEL, D_EXPERT), D_MODEL ** -0.5),
        "w_e2": nrm(ks[19], (DEPTH, N_EXPERTS, D_EXPERT, D_MODEL), D_EXPERT ** -0.5 * BETA),
        "ln2_g": 1.0 + nrm(ks[20], (DEPTH, D_MODEL), 0.02),
        "ln2_b": nrm(ks[21], (DEPTH, D_MODEL), 0.02),
    }


def reference(x_prompt, x_sample, ln_in_g, ln_in_b, w_in, dw_w, dw_b, conv_ln_g, conv_ln_b,
              w_conv_out, hg_lower, hg_norm_g, w_hg_out, w_o, ln1_g, ln1_b, w_router, b_router,
              w_e1, w_e2, ln2_g, ln2_b):
    y_prompt = _trunk(x_prompt, ln_in_g, ln_in_b, w_in, dw_w, dw_b, conv_ln_g, conv_ln_b,
                      w_conv_out, hg_lower, hg_norm_g, w_hg_out, w_o, ln1_g, ln1_b,
                      w_router, b_router, w_e1, w_e2, ln2_g, ln2_b)
    y_sample = _trunk(x_sample, ln_in_g, ln_in_b, w_in, dw_w, dw_b, conv_ln_g, conv_ln_b,
                      w_conv_out, hg_lower, hg_norm_g, w_hg_out, w_o, ln1_g, ln1_b,
                      w_router, b_router, w_e1, w_e2, ln2_g, ln2_b)
    return (y_prompt, y_sample)
```

```python
import functools

import numpy as np
import jax
import jax.numpy as jnp
from jax import lax
from jax.experimental import pallas as pl
from jax.experimental.pallas import tpu as pltpu

F32 = jnp.float32
BF16 = jnp.bfloat16

D_MODEL = 1024
DEPTH = 4
HEAD_DIM = 128
N_HEADS = D_MODEL // HEAD_DIM
CONV_WIDTH = 31
CONV_PAD = CONV_WIDTH // 2
HALO = 16
CHUNK = 64
EXP_CLAMP = 60.0
N_EXPERTS = 16
N_GROUPS = 4
EXPERTS_PER_GROUP = 4
D_EXPERT = 2 * D_MODEL
PAIRS = ((0, 1), (0, 2), (0, 3), (1, 2), (1, 3), (2, 3))
N_CLASSES = N_GROUPS * len(PAIRS)
ALPHA = (2 * DEPTH) ** 0.25
LN_EPS = 1e-5
RMS_EPS = 1e-6
VMEM_LIMIT = 56 * 1024 * 1024


def _params(sem, vmem=None):
    return pltpu.CompilerParams(dimension_semantics=sem, vmem_limit_bytes=vmem)


def _sigmoid(x):
    return 1.0 / (1.0 + jnp.exp(-x))


def _layer_norm_rows(x, g, b):
    mu = jnp.mean(x, axis=-1, keepdims=True)
    xc = x - mu
    var = jnp.mean(xc * xc, axis=-1, keepdims=True)
    return xc * lax.rsqrt(var + LN_EPS) * g + b


def _ln_kernel(x_ref, g_ref, b_ref, o_ref):
    o_ref[...] = _layer_norm_rows(x_ref[...], g_ref[...], b_ref[...])


def _ln_call(x, g, b, tm):
    n = x.shape[0]
    row = pl.BlockSpec((tm, D_MODEL), lambda i: (i, 0))
    vec = pl.BlockSpec((1, D_MODEL), lambda i: (0, 0))
    return pl.pallas_call(
        _ln_kernel, grid=(n // tm,), in_specs=[row, vec, vec], out_specs=row,
        out_shape=jax.ShapeDtypeStruct((n, D_MODEL), F32),
        compiler_params=_params(("parallel",)), name="ln_in",
    )(x, g.reshape(1, -1), b.reshape(1, -1))


def _res_ln_kernel(h_ref, y_ref, g_ref, b_ref, o_ref):
    o_ref[...] = _layer_norm_rows(ALPHA * h_ref[...] + y_ref[...], g_ref[...], b_ref[...])


def _res_ln_call(h, y, g, b, tm):
    n = h.shape[0]
    row = pl.BlockSpec((tm, D_MODEL), lambda i: (i, 0))
    vec = pl.BlockSpec((1, D_MODEL), lambda i: (0, 0))
    return pl.pallas_call(
        _res_ln_kernel, grid=(n // tm,), in_specs=[row, row, vec, vec], out_specs=row,
        out_shape=jax.ShapeDtypeStruct((n, D_MODEL), F32),
        compiler_params=_params(("parallel",)), name="res_ln",
    )(h, y, g.reshape(1, -1), b.reshape(1, -1))


def _heads_store(ref, val):
    for hd in range(N_HEADS):
        ref[hd] = val[:, hd * HEAD_DIM:(hd + 1) * HEAD_DIM].astype(ref.dtype)


def _proj_kernel(x_ref, wv_ref, w_ref, lb_ref, u_ref, qvo_ref, f_ref, gab_ref):
    j = pl.program_id(1)
    x = x_ref[...].astype(BF16)
    acc = jnp.dot(x, w_ref[...], preferred_element_type=F32)

    @pl.when(j == 0)
    def _():
        val = jnp.dot(x, wv_ref[...], preferred_element_type=F32)
        u_ref[...] = (val * _sigmoid(acc)).astype(BF16)

    @pl.when(j == 1)
    def _():
        _heads_store(qvo_ref, acc * _sigmoid(acc) * HEAD_DIM ** -0.5)

    @pl.when(j == 2)
    def _():
        _heads_store(qvo_ref, acc)

    @pl.when(j == 3)
    def _():
        _heads_store(qvo_ref, acc * _sigmoid(acc))

    @pl.when((j == 4) | (j == 5))
    def _():
        en = jnp.exp(-jnp.abs(acc))
        sig = jnp.where(acc >= 0.0, 1.0, en) / (1.0 + en)
        e = jnp.exp(jnp.minimum(-acc, EXP_CLAMP))
        _heads_store(f_ref, jnp.minimum(sig * (1.0 + lb_ref[...] * e), 1.0))

    @pl.when(j >= 6)
    def _():
        gab_ref[...] = _sigmoid(acc).astype(BF16)


def _proj_call(h, w, lb, tm):
    n = h.shape[0]
    return pl.pallas_call(
        _proj_kernel, grid=(n // tm, 8),
        in_specs=[
            pl.BlockSpec((tm, D_MODEL), lambda i, j: (i, 0)),
            pl.BlockSpec((D_MODEL, D_MODEL), lambda i, j: (0, 0)),
            pl.BlockSpec((D_MODEL, D_MODEL), lambda i, j: (0, j + 1)),
            pl.BlockSpec((None, 1, D_MODEL), lambda i, j: (jnp.clip(j - 4, 0, 1), 0, 0)),
        ],
        out_specs=[
            pl.BlockSpec((tm, D_MODEL), lambda i, j: (i, 0)),
            pl.BlockSpec((None, N_HEADS, tm, HEAD_DIM), lambda i, j: (jnp.clip(j - 1, 0, 2), 0, i, 0)),
            pl.BlockSpec((None, N_HEADS, tm, HEAD_DIM), lambda i, j: (jnp.clip(j - 4, 0, 1), 0, i, 0)),
            pl.BlockSpec((None, tm, D_MODEL), lambda i, j: (jnp.clip(j - 6, 0, 1), i, 0)),
        ],
        out_shape=[
            jax.ShapeDtypeStruct((n, D_MODEL), BF16),
            jax.ShapeDtypeStruct((3, N_HEADS, n, HEAD_DIM), BF16),
            jax.ShapeDtypeStruct((2, N_HEADS, n, HEAD_DIM), F32),
            jax.ShapeDtypeStruct((2, n, D_MODEL), BF16),
        ],
        compiler_params=_params(("parallel", "arbitrary"), VMEM_LIMIT), name="proj",
    )(h, w, w, lb)


def _conv_kernel(first_ref, last_ref, up_ref, u_ref, un_ref, w_ref, b_ref, g_ref, beta_ref, o_ref,
                 buf_ref, acc_ref, *, rows):
    i = pl.program_id(0)
    t = u_ref.shape[0]
    nslab = D_MODEL // 128
    prev = jnp.where(first_ref[i] == 1, 0.0, up_ref[...].astype(F32))
    nxt = jnp.where(last_ref[i] == 1, 0.0, un_ref[...].astype(F32))
    for c in range(nslab):
        cols = slice(c * 128, (c + 1) * 128)
        buf_ref[c, 0:HALO, :] = prev[:, cols]
        buf_ref[c, HALO:HALO + t, :] = u_ref[:, cols].astype(F32)
        buf_ref[c, HALO + t:2 * HALO + t, :] = nxt[:, cols]

    def slab(c, carry):
        for rb in range(t // rows):
            acc = jnp.zeros((rows, 128), F32)
            for k in range(CONV_WIDTH):
                r0 = rb * rows + HALO - CONV_PAD + k
                acc = acc + w_ref[c, k:k + 1, :] * buf_ref[c, r0:r0 + rows, :]
            acc_ref[c, rb * rows:(rb + 1) * rows, :] = acc
        return carry

    lax.fori_loop(0, nslab, slab, 0)
    x = jnp.concatenate([acc_ref[c] for c in range(nslab)], axis=-1) + b_ref[...]
    y = _layer_norm_rows(x, g_ref[...], beta_ref[...])
    o_ref[...] = (y * _sigmoid(y)).astype(BF16)


def _conv_call(u, first, last, w, b, g, beta, t):
    n = u.shape[0]
    nb, hb = n // t, t // HALO
    nslab = D_MODEL // 128
    wpad = jnp.zeros((32, D_MODEL), F32).at[:CONV_WIDTH].set(w).reshape(32, nslab, 128).transpose(1, 0, 2)
    vec = pl.BlockSpec((1, D_MODEL), lambda i, *_: (0, 0))
    return pl.pallas_call(
        functools.partial(_conv_kernel, rows=32),
        grid_spec=pltpu.PrefetchScalarGridSpec(
            num_scalar_prefetch=2, grid=(nb,),
            in_specs=[
                pl.BlockSpec((HALO, D_MODEL), lambda i, *_: (jnp.maximum(i * hb - 1, 0), 0)),
                pl.BlockSpec((t, D_MODEL), lambda i, *_: (i, 0)),
                pl.BlockSpec((HALO, D_MODEL), lambda i, *_: (jnp.minimum((i + 1) * hb, nb * hb - 1), 0)),
                pl.BlockSpec((nslab, 32, 128), lambda i, *_: (0, 0, 0)),
                vec, vec, vec,
            ],
            out_specs=pl.BlockSpec((t, D_MODEL), lambda i, *_: (i, 0)),
            scratch_shapes=[pltpu.VMEM((nslab, t + 2 * HALO, 128), F32), pltpu.VMEM((nslab, t, 128), F32)],
        ),
        out_shape=jax.ShapeDtypeStruct((n, D_MODEL), BF16),
        compiler_params=_params(("parallel",), VMEM_LIMIT), name="conv",
    )(first, last, u, u, u, wpad, b.reshape(1, -1), g.reshape(1, -1), beta.reshape(1, -1))


def _mul(a, b):
    if a is None:
        return b
    if b is None:
        return a
    return a * b


def _cat(tiles):
    return jnp.concatenate(tiles, axis=0)


def _nt(a, b):
    return lax.dot_general(a, b, (((1,), (1,)), ((), ())), preferred_element_type=F32)


def _tn(a, b):
    return lax.dot_general(a, b, (((0,), (0,)), ((), ())), preferred_element_type=F32)


def _gla_masks():
    i = lax.broadcasted_iota(jnp.int32, (CHUNK, CHUNK), 0)
    j = lax.broadcasted_iota(jnp.int32, (CHUNK, CHUNK), 1)
    pt, rt, ps, rs = i >> 3, i & 7, j >> 3, j & 7
    masks = [i == j]
    for lvl in range(3):
        masks.append((rt == rs) & ((pt >> (lvl + 1)) == (ps >> (lvl + 1))))
    for lvl in range(3):
        masks.append((rt >> (lvl + 1)) == (rs >> (lvl + 1)))
    return masks


def _block_products(tot, rev):
    ri = lax.broadcasted_iota(jnp.int32, tot.shape, 0)
    r = (7 - ri) if rev else ri

    def sh(x, d):
        return pltpu.roll(x, ((-d) if rev else d) % 8, 0)

    one = jnp.ones_like(tot)
    zero = jnp.zeros_like(tot)
    b0, b1, b2 = (r & 1) == 1, (r & 2) == 2, (r & 4) == 4
    m4 = r & 3
    d1, u1 = sh(tot, 1), sh(tot, -1)

    def excl_prefix(width):
        m = r & (width - 1)
        e = jnp.where(m >= 1, d1, one)
        step = 1
        while step < width:
            e = e * jnp.where(m >= step, sh(e, step), one)
            step *= 2
        return e

    def excl_suffix(width):
        m = r & (width - 1)
        e = jnp.where(m <= width - 2, u1, one)
        step = 1
        while step < width:
            e = e * jnp.where(m <= width - 1 - step, sh(e, -step), one)
            step *= 2
        return e

    lvl3 = (jnp.where(b0, one, zero), jnp.where(b0, zero, one))
    lvl4 = (jnp.where(b1, jnp.where(b0, d1, one), zero), jnp.where(b1, zero, jnp.where(b0, one, u1)))
    lvl5 = (jnp.where(b2, excl_prefix(4), zero), jnp.where(b2, zero, excl_suffix(4)))
    state = (excl_prefix(8), excl_suffix(8))
    del m4
    return [lvl3, lvl4, lvl5, state]


def _gla_chunk(fs, qs, vs, st, masks, rev):
    n = 8
    ks = [1.0 - f for f in fs]
    fi = [list(fs)]
    fi.append([fs[a] * fs[a - 1] if a & 1 else fs[a] for a in range(n)])
    fi.append([fi[1][a] * fi[1][(a & ~3) + 1] if a & 2 else fi[1][a] for a in range(n)])
    fi.append([fi[2][a] * fi[2][3] if a & 4 else fi[2][a] for a in range(n)])
    ge = [[None] * n]
    ge.append([fs[a + 1] if not a & 1 else None for a in range(n)])
    ge.append([_mul(ge[1][a], fi[1][(a & ~3) + 3]) if not a & 2 else ge[1][a] for a in range(n)])
    ge.append([_mul(ge[2][a], fi[2][7]) if not a & 4 else ge[2][a] for a in range(n)])

    zero = jnp.zeros_like(fs[0])
    vb = _cat(vs).astype(BF16)
    att = jnp.where(masks[0], _nt(_cat(qs).astype(BF16), _cat(ks).astype(BF16)), 0.0)
    for lvl in range(3):
        qt = [qs[a] * fi[lvl][a] if (a >> lvl) & 1 else zero for a in range(n)]
        kt = [zero if (a >> lvl) & 1 else _mul(ks[a], ge[lvl][a]) for a in range(n)]
        att = att + jnp.where(masks[1 + lvl], _nt(_cat(qt).astype(BF16), _cat(kt).astype(BF16)), 0.0)

    qf = [qs[a] * fi[3][a] for a in range(n)]
    kg = [_mul(ks[a], ge[3][a]) for a in range(n)]
    tot = fi[3][7]
    facs = _block_products(tot, rev)
    for lvl in range(3):
        rq, rk = facs[lvl]
        qt = _cat([x * rq for x in qf]).astype(BF16)
        kt = _cat([x * rk for x in kg]).astype(BF16)
        att = att + jnp.where(masks[4 + lvl], _nt(qt, kt), 0.0)
    rq, rk = facs[3]
    q_in = _cat([x * rq for x in qf]).astype(BF16)
    k_out = _cat([x * rk for x in kg]).astype(BF16)
    out = jnp.dot(att.astype(BF16), vb, preferred_element_type=F32) + _nt(q_in, st.astype(BF16))
    last = 0 if rev else 7
    decay = (rq * tot)[last:last + 1, :]
    st = st * decay + _tn(vb, k_out)
    return [out[8 * a:8 * a + 8] for a in range(n)], st


def _gla_kernel(flag_ref, q_ref, v_ref, f_ref, *rest, rev, final):
    if final:
        obw_ref, og_ref, g_ref, o_ref, st_ref, qs_ref, vs_ref, os_ref = rest
    else:
        o_ref, st_ref, qs_ref, vs_ref = rest
        os_ref = o_ref
    nb = pl.num_programs(1)
    tb = pl.program_id(1)
    blk = (nb - 1 - tb) if rev else tb
    t = q_ref.shape[0]
    nchunks = t // CHUNK

    @pl.when(flag_ref[blk] == 1)
    def _():
        st_ref[...] = jnp.zeros_like(st_ref)

    qs_ref[...] = q_ref[...].astype(F32)
    vs_ref[...] = v_ref[...].astype(F32)
    masks = _gla_masks()
    order = list(range(7, -1, -1)) if rev else list(range(8))

    def chunk(ci, carry):
        c = (nchunks - 1 - ci) if rev else ci
        c0 = pl.multiple_of(c * CHUNK, CHUNK)
        rows = [pl.ds(c0 + a, 8, stride=8) for a in order]
        fs = [f_ref[rw, :] for rw in rows]
        qs = [qs_ref[rw, :] for rw in rows]
        vs = [vs_ref[rw, :] for rw in rows]
        outs, st = _gla_chunk(fs, qs, vs, st_ref[...], masks, rev)
        st_ref[...] = st
        for rw, o in zip(rows, outs):
            os_ref[rw, :] = o
        return carry

    lax.fori_loop(0, nchunks, chunk, 0)

    if final:
        o = os_ref[...] + obw_ref[...]
        o = o * lax.rsqrt(jnp.mean(o * o, axis=-1, keepdims=True) + RMS_EPS) * g_ref[...]
        o_ref[...] = (o * og_ref[...].astype(F32)).astype(BF16)


def _gla_call(flags, qvo, f, direction, t, obw=None, g=None):
    n = qvo.shape[2]
    nb = n // t
    rev = direction == 1
    final = not rev

    def tmap(hd, tb, *_):
        return (nb - 1 - tb) if rev else tb

    def sel(k):
        return pl.BlockSpec((None, None, t, HEAD_DIM), lambda hd, tb, *_: (k, hd, tmap(hd, tb), 0))

    in_specs = [sel(0), sel(1), sel(direction)]
    args = [qvo, qvo, f]
    scratch = [pltpu.VMEM((HEAD_DIM, HEAD_DIM), F32), pltpu.VMEM((t, HEAD_DIM), F32),
               pltpu.VMEM((t, HEAD_DIM), F32)]
    if final:
        in_specs += [pl.BlockSpec((None, t, HEAD_DIM), lambda hd, tb, *_: (hd, tb, 0)), sel(2),
                     pl.BlockSpec((None, 1, HEAD_DIM), lambda hd, tb, *_: (hd, 0, 0))]
        args += [obw, qvo, g.reshape(N_HEADS, 1, HEAD_DIM)]
        scratch.append(pltpu.VMEM((t, HEAD_DIM), F32))
    return pl.pallas_call(
        functools.partial(_gla_kernel, rev=rev, final=final),
        grid_spec=pltpu.PrefetchScalarGridSpec(
            num_scalar_prefetch=1, grid=(N_HEADS, nb), in_specs=in_specs,
            out_specs=pl.BlockSpec((None, t, HEAD_DIM), lambda hd, tb, *_: (hd, tmap(hd, tb), 0)),
            scratch_shapes=scratch,
        ),
        out_shape=jax.ShapeDtypeStruct((N_HEADS, n, HEAD_DIM), BF16 if final else F32),
        compiler_params=_params(("parallel", "arbitrary"), VMEM_LIMIT),
        name="gla_fwd" if final else "gla_bwd",
    )(flags, *args)


def _merge_kernel(h_ref, ua_ref, ob_ref, gab_ref, wc_ref, whg_ref, wo_ref, g_ref, b_ref, o_ref):
    ob = jnp.concatenate([ob_ref[hd] for hd in range(N_HEADS)], axis=-1)
    branch_a = jnp.dot(ua_ref[...], wc_ref[...], preferred_element_type=F32)
    branch_b = jnp.dot(ob, whg_ref[...], preferred_element_type=F32)
    merged = gab_ref[0].astype(F32) * branch_a + gab_ref[1].astype(F32) * branch_b
    mix = jnp.dot(merged.astype(BF16), wo_ref[...], preferred_element_type=F32)
    o_ref[...] = _layer_norm_rows(ALPHA * h_ref[...] + mix, g_ref[...], b_ref[...])


def _merge_call(h, ua, ob, gab, wc, whg, wo, g, b, tm):
    n = h.shape[0]
    row = pl.BlockSpec((tm, D_MODEL), lambda i: (i, 0))
    mat = pl.BlockSpec((D_MODEL, D_MODEL), lambda i: (0, 0))
    vec = pl.BlockSpec((1, D_MODEL), lambda i: (0, 0))
    return pl.pallas_call(
        _merge_kernel, grid=(n // tm,),
        in_specs=[row, row, pl.BlockSpec((N_HEADS, tm, HEAD_DIM), lambda i: (0, i, 0)),
                  pl.BlockSpec((2, tm, D_MODEL), lambda i: (0, i, 0)), mat, mat, mat, vec, vec],
        out_specs=row, out_shape=jax.ShapeDtypeStruct((n, D_MODEL), F32),
        compiler_params=_params(("parallel",), VMEM_LIMIT), name="merge",
    )(h, ua, ob, gab, wc, whg, wo, g.reshape(1, -1), b.reshape(1, -1))


def _router_kernel(h_ref, w_ref, b_ref, cls_ref, gate_ref):
    logits = lax.dot_general(w_ref[...], h_ref[...], (((1,), (1,)), ((), ())),
                             precision=lax.Precision.HIGHEST, preferred_element_type=F32) + b_ref[...]
    rows = [logits[e:e + 1, :] for e in range(N_EXPERTS)]
    mx = functools.reduce(jnp.maximum, rows)
    ex = [jnp.exp(x - mx) for x in rows]
    den = functools.reduce(lambda a, b: a + b, ex)
    sc = [e / den for e in ex]
    gscore = []
    for grp in range(N_GROUPS):
        s = sc[grp * 4:grp * 4 + 4]
        gscore.append(functools.reduce(jnp.maximum, [s[a] + s[b] for a, b in PAIRS]))
    gsel = jnp.zeros_like(gscore[0], dtype=jnp.int32)
    best = gscore[0]
    for grp in range(1, N_GROUPS):
        better = gscore[grp] > best
        gsel = jnp.where(better, grp, gsel)
        best = jnp.where(better, gscore[grp], best)
    ing = []
    for k in range(EXPERTS_PER_GROUP):
        x = sc[k]
        for grp in range(1, N_GROUPS):
            x = jnp.where(gsel == grp, sc[grp * 4 + k], x)
        ing.append(x)
    i1 = jnp.zeros_like(gsel)
    w1 = ing[0]
    for k in range(1, 4):
        better = ing[k] > w1
        i1 = jnp.where(better, k, i1)
        w1 = jnp.where(better, ing[k], w1)
    i2 = jnp.full_like(gsel, -1)
    w2 = jnp.full_like(w1, -1.0)
    for k in range(4):
        better = (i1 != k) & (ing[k] > w2)
        i2 = jnp.where(better, k, i2)
        w2 = jnp.where(better, ing[k], w2)
    tot = w1 + w2
    g1, g2 = w1 / tot, w2 / tot
    lo, hi = jnp.minimum(i1, i2), jnp.maximum(i1, i2)
    pair = jnp.where(lo == 0, hi - 1, jnp.where(lo == 1, hi + 1, 5))
    cls_ref[...] = gsel * len(PAIRS) + pair
    first_is_lo = i1 < i2
    gate_ref[0:1, :] = jnp.where(first_is_lo, g1, g2)
    gate_ref[1:2, :] = jnp.where(first_is_lo, g2, g1)


def _router_call(h, w_t, b, tm):
    n = h.shape[0]
    return pl.pallas_call(
        _router_kernel, grid=(n // tm,),
        in_specs=[pl.BlockSpec((tm, D_MODEL), lambda i: (i, 0)),
                  pl.BlockSpec((N_EXPERTS, D_MODEL), lambda i: (0, 0)),
                  pl.BlockSpec((N_EXPERTS, 1), lambda i: (0, 0))],
        out_specs=[pl.BlockSpec((1, tm), lambda i: (0, i)), pl.BlockSpec((2, tm), lambda i: (0, i))],
        out_shape=[jax.ShapeDtypeStruct((1, n), jnp.int32), jax.ShapeDtypeStruct((2, n), F32)],
        compiler_params=_params(("parallel",)), name="router",
    )(h, w_t, b)


def _gather_kernel(idx_ref, x_ref, o_ref, sem):
    tm = o_ref.shape[0]
    base = pl.program_id(0) * tm

    def row_copy(r):
        return pltpu.make_async_copy(x_ref.at[pl.ds(idx_ref[base + r], 1), :], o_ref.at[pl.ds(r, 1), :], sem)

    def start(r, carry):
        row_copy(r).start()
        return carry

    def wait(r, carry):
        row_copy(r).wait()
        return carry

    lax.fori_loop(0, tm, start, 0)
    lax.fori_loop(0, tm, wait, 0)


def _gather_call(x, idx, tm):
    n_out = idx.shape[0]
    return pl.pallas_call(
        _gather_kernel,
        grid_spec=pltpu.PrefetchScalarGridSpec(
            num_scalar_prefetch=1, grid=(n_out // tm,),
            in_specs=[pl.BlockSpec(memory_space=pl.ANY)],
            out_specs=pl.BlockSpec((tm, D_MODEL), lambda i, *_: (i, 0)),
            scratch_shapes=[pltpu.SemaphoreType.DMA(())],
        ),
        out_shape=jax.ShapeDtypeStruct((n_out, D_MODEL), x.dtype),
        compiler_params=_params(("arbitrary",)), name="gather",
    )(idx, x)


def _gelu_tanh(x):
    return 0.5 * x * (1.0 + jnp.tanh(0.7978845608028654 * (x + 0.044715 * x * x * x)))


def _moe_kernel(ea_ref, eb_ref, used_ref, x_ref, g_ref, w1a_ref, w2a_ref, w1b_ref, w2b_ref, o_ref):
    @pl.when(pl.program_id(0) < used_ref[0])
    def _():
        x = x_ref[...].astype(BF16)

        def expert(w1_ref, w2_ref):
            hid = _gelu_tanh(jnp.dot(x, w1_ref[...], preferred_element_type=F32))
            return jnp.dot(hid.astype(BF16), w2_ref[...], preferred_element_type=F32)

        g = g_ref[...]
        o_ref[...] = g[:, 0:1] * expert(w1a_ref, w2a_ref) + g[:, 1:2] * expert(w1b_ref, w2b_ref)

    @pl.when(pl.program_id(0) >= used_ref[0])
    def _():
        o_ref[...] = jnp.zeros_like(o_ref)


def _moe_call(xs, gates, blk_ea, blk_eb, n_used, w1, w2, tm):
    npad = xs.shape[0]
    w1a = pl.BlockSpec((None, D_MODEL, D_EXPERT), lambda i, ea, eb, u: (ea[i], 0, 0))
    w2a = pl.BlockSpec((None, D_EXPERT, D_MODEL), lambda i, ea, eb, u: (ea[i], 0, 0))
    w1b = pl.BlockSpec((None, D_MODEL, D_EXPERT), lambda i, ea, eb, u: (eb[i], 0, 0))
    w2b = pl.BlockSpec((None, D_EXPERT, D_MODEL), lambda i, ea, eb, u: (eb[i], 0, 0))
    return pl.pallas_call(
        _moe_kernel,
        grid_spec=pltpu.PrefetchScalarGridSpec(
            num_scalar_prefetch=3, grid=(npad // tm,),
            in_specs=[pl.BlockSpec((tm, D_MODEL), lambda i, *_: (i, 0)),
                      pl.BlockSpec((tm, 2), lambda i, *_: (i, 0)), w1a, w2a, w1b, w2b],
            out_specs=pl.BlockSpec((tm, D_MODEL), lambda i, *_: (i, 0)),
        ),
        out_shape=jax.ShapeDtypeStruct((npad, D_MODEL), F32),
        compiler_params=_params(("arbitrary",), VMEM_LIMIT), name="moe",
    )(blk_ea, blk_eb, n_used, xs, gates, w1, w2, w1, w2)


def _routing_tables(cls, gates, tm):
    n = cls.shape[0]
    nblk = n // tm + N_CLASSES
    counts = jnp.zeros((N_CLASSES,), jnp.int32).at[cls].add(1)
    starts = jnp.cumsum(counts) - counts
    padded = (counts + tm - 1) // tm * tm
    pad_end = jnp.cumsum(padded)
    order = jnp.argsort(cls, stable=True).astype(jnp.int32)
    c_sorted = cls[order]
    dest_sorted = pad_end[c_sorted] - padded[c_sorted] + jnp.arange(n, dtype=jnp.int32) - starts[c_sorted]
    src = jnp.zeros((nblk * tm,), jnp.int32).at[dest_sorted].set(order)
    dest = jnp.zeros((n,), jnp.int32).at[order].set(dest_sorted)
    gates_sorted = jnp.zeros((nblk * tm, 2), F32).at[dest].set(gates)
    n_used = pad_end[-1] // tm
    blk = jnp.arange(nblk, dtype=jnp.int32)
    blk_cls = jnp.searchsorted(pad_end, jnp.minimum(blk, n_used - 1) * tm, side="right").astype(jnp.int32)
    blk_cls = jnp.minimum(blk_cls, N_CLASSES - 1)
    pairs = jnp.asarray(PAIRS, jnp.int32)
    grp, pr = blk_cls // len(PAIRS), blk_cls % len(PAIRS)
    blk_ea = grp * EXPERTS_PER_GROUP + pairs[pr, 0]
    blk_eb = grp * EXPERTS_PER_GROUP + pairs[pr, 1]
    return src, dest, gates_sorted, blk_ea, blk_eb, n_used.reshape(1).astype(jnp.int32)


def _block_flags(seq_lens, t):
    first, last = [], []
    for length in seq_lens:
        nb = length // t
        first += [1] + [0] * (nb - 1)
        last += [0] * (nb - 1) + [1]
    return jnp.asarray(np.array(first, np.int32)), jnp.asarray(np.array(last, np.int32))


def _trunk(x, seq_lens, p, *, tm, t_conv, t_gla, tm_moe):
    (ln_in_g, ln_in_b, w_in, dw_w, dw_b, conv_ln_g, conv_ln_b, w_conv_out, hg_lower, hg_norm_g, w_hg_out,
     w_o, ln1_g, ln1_b, w_router, b_router, w_e1, w_e2, ln2_g, ln2_b) = p
    depth = w_in.shape[0]
    conv_first, conv_last = _block_flags(seq_lens, t_conv)
    gla_first, gla_last = _block_flags(seq_lens, t_gla)

    lb_p = jax.nn.softmax(hg_lower.astype(F32), axis=0)
    lb_all = jnp.clip(jnp.cumsum(lb_p, axis=0) - lb_p[0:1], 0.0, 1.0)
    w_in_g = w_in.reshape(depth, D_MODEL, 9, D_MODEL)[:, :, jnp.asarray([0, 1, 2, 5, 6, 3, 4, 7, 8])]
    w_in_b = w_in_g.reshape(depth, D_MODEL, 9 * D_MODEL).astype(BF16)
    w_router_t = w_router.T.astype(F32)
    b_router_c = b_router.reshape(N_EXPERTS, 1).astype(F32)

    h = _ln_call(x, ln_in_g, ln_in_b, tm)
    for l in range(depth):
        u, qvo, f, gab = _proj_call(h, w_in_b[l], lb_all[l].reshape(2, 1, D_MODEL), tm)
        ua = _conv_call(u, conv_first, conv_last, dw_w[l], dw_b[l], conv_ln_g[l], conv_ln_b[l], t_conv)
        o_bw = _gla_call(gla_last, qvo, f, 1, t_gla)
        ob = _gla_call(gla_first, qvo, f, 0, t_gla, obw=o_bw, g=hg_norm_g[l].astype(F32))
        h1 = _merge_call(h, ua, ob, gab, w_conv_out[l].astype(BF16), w_hg_out[l].astype(BF16),
                         w_o[l].astype(BF16), ln1_g[l], ln1_b[l], tm)
        cls, gates = _router_call(h1, w_router_t, b_router_c, tm)
        src, dest, gates_sorted, blk_ea, blk_eb, n_used = _routing_tables(cls[0], gates.T, tm_moe)
        xs = _gather_call(h1, src, tm_moe)
        ys = _moe_call(xs, gates_sorted, blk_ea, blk_eb, n_used, w_e1[l].astype(BF16), w_e2[l].astype(BF16),
                       tm_moe)
        y = _gather_call(ys, dest, tm_moe)
        h = _res_ln_call(h1, y, ln2_g[l], ln2_b[l], tm)
    return h


def kernel(x_prompt, x_sample, ln_in_g, ln_in_b, w_in, dw_w, dw_b, conv_ln_g, conv_ln_b, w_conv_out, hg_lower,
           hg_norm_g, w_hg_out, w_o, ln1_g, ln1_b, w_router, b_router, w_e1, w_e2, ln2_g, ln2_b):
    bp, lp, d = x_prompt.shape
    bs, ls, _ = x_sample.shape
    x = jnp.concatenate([x_prompt.reshape(bp * lp, d), x_sample.reshape(bs * ls, d)], axis=0)
    seq_lens = (lp,) * bp + (ls,) * bs
    p = (ln_in_g, ln_in_b, w_in, dw_w, dw_b, conv_ln_g, conv_ln_b, w_conv_out, hg_lower, hg_norm_g, w_hg_out,
         w_o, ln1_g, ln1_b, w_router, b_router, w_e1, w_e2, ln2_g, ln2_b)
    h = _trunk(x, seq_lens, p, tm=512, t_conv=512, t_gla=2048, tm_moe=256)
    return (h[:bp * lp].reshape(bp, lp, d), h[bp * lp:].reshape(bs, ls, d))
```

```python
import functools
import math

import numpy as np
import jax
import jax.numpy as jnp
from jax import lax
from jax.experimental import pallas as pl
from jax.experimental.pallas import tpu as pltpu

F32 = jnp.float32
BF16 = jnp.bfloat16

D_MODEL = 1024
DEPTH = 4
HEAD_DIM = 128
N_HEADS = D_MODEL // HEAD_DIM
CONV_WIDTH = 31
CONV_PAD = CONV_WIDTH // 2
HALO = 16
CHUNK = 64
GLA_GROUP = 4
EXP_CLAMP = 60.0
N_EXPERTS = 16
N_GROUPS = 4
EXPERTS_PER_GROUP = 4
D_EXPERT = 2 * D_MODEL
PAIRS = ((0, 1), (0, 2), (0, 3), (1, 2), (1, 3), (2, 3))
N_CLASSES = N_GROUPS * len(PAIRS)
ALPHA = (2 * DEPTH) ** 0.25
LN_EPS = 1e-5
RMS_EPS = 1e-6
VMEM_LIMIT = 56 * 1024 * 1024


def _params(sem, vmem=None):
    return pltpu.CompilerParams(dimension_semantics=sem, vmem_limit_bytes=vmem)


def _sigmoid(x):
    return 1.0 / (1.0 + jnp.exp(-x))


def _layer_norm_rows(x, g, b):
    mu = jnp.mean(x, axis=-1, keepdims=True)
    xc = x - mu
    var = jnp.mean(xc * xc, axis=-1, keepdims=True)
    return xc * lax.rsqrt(var + LN_EPS) * g + b


def _ln_kernel(x_ref, g_ref, b_ref, o_ref):
    o_ref[...] = _layer_norm_rows(x_ref[...], g_ref[...], b_ref[...])


def _ln_call(x, g, b, tm):
    n = x.shape[0]
    row = pl.BlockSpec((tm, D_MODEL), lambda i: (i, 0))
    vec = pl.BlockSpec((1, D_MODEL), lambda i: (0, 0))
    return pl.pallas_call(
        _ln_kernel, grid=(n // tm,), in_specs=[row, vec, vec], out_specs=row,
        out_shape=jax.ShapeDtypeStruct((n, D_MODEL), F32),
        compiler_params=_params(("parallel",)), name="ln_in",
    )(x, g.reshape(1, -1), b.reshape(1, -1))


def _res_ln_kernel(h_ref, y_ref, g_ref, b_ref, o_ref):
    o_ref[...] = _layer_norm_rows(ALPHA * h_ref[...] + y_ref[...], g_ref[...], b_ref[...])


def _res_ln_call(h, y, g, b, tm):
    n = h.shape[0]
    row = pl.BlockSpec((tm, D_MODEL), lambda i: (i, 0))
    vec = pl.BlockSpec((1, D_MODEL), lambda i: (0, 0))
    return pl.pallas_call(
        _res_ln_kernel, grid=(n // tm,), in_specs=[row, row, vec, vec], out_specs=row,
        out_shape=jax.ShapeDtypeStruct((n, D_MODEL), F32),
        compiler_params=_params(("parallel",)), name="res_ln",
    )(h, y, g.reshape(1, -1), b.reshape(1, -1))


PROJ_COLS = 256


def _proj_kernel(x_ref, wv_ref, w_ref, lb_ref, u_ref, qvo_ref, f_ref, gab_ref):
    j = pl.program_id(1)
    x = x_ref[...].astype(BF16)
    pieces = [slice(c * PROJ_COLS, (c + 1) * PROJ_COLS) for c in range(D_MODEL // PROJ_COLS)]

    def mm(ref, cols):
        return jnp.dot(x, ref[:, cols], preferred_element_type=F32)

    def heads_store(ref, cols, val):
        for k in range(PROJ_COLS // HEAD_DIM):
            ref[cols.start // HEAD_DIM + k] = val[:, k * HEAD_DIM:(k + 1) * HEAD_DIM].astype(ref.dtype)

    @pl.when(j == 0)
    def _():
        for cols in pieces:
            u_ref[:, cols] = (mm(wv_ref, cols) * _sigmoid(mm(w_ref, cols))).astype(BF16)

    @pl.when(j == 1)
    def _():
        for cols in pieces:
            a = mm(w_ref, cols)
            heads_store(qvo_ref, cols, a * _sigmoid(a) * HEAD_DIM ** -0.5)

    @pl.when(j == 2)
    def _():
        for cols in pieces:
            heads_store(qvo_ref, cols, mm(w_ref, cols))

    @pl.when(j == 3)
    def _():
        for cols in pieces:
            a = mm(w_ref, cols)
            heads_store(qvo_ref, cols, a * _sigmoid(a))

    @pl.when((j == 4) | (j == 5))
    def _():
        for cols in pieces:
            t = jnp.exp(-mm(w_ref, cols))
            f = (1.0 + lb_ref[:, cols] * jnp.minimum(t, math.exp(EXP_CLAMP))) / (1.0 + t)
            heads_store(f_ref, cols, jnp.minimum(f, 1.0))

    @pl.when(j >= 6)
    def _():
        for cols in pieces:
            gab_ref[:, cols] = _sigmoid(mm(w_ref, cols)).astype(BF16)


def _proj_call(h, w, lb, tm):
    n = h.shape[0]
    return pl.pallas_call(
        _proj_kernel, grid=(n // tm, 8),
        in_specs=[
            pl.BlockSpec((tm, D_MODEL), lambda i, j: (i, 0)),
            pl.BlockSpec((D_MODEL, D_MODEL), lambda i, j: (0, 0)),
            pl.BlockSpec((D_MODEL, D_MODEL), lambda i, j: (0, j + 1)),
            pl.BlockSpec((None, 1, D_MODEL), lambda i, j: (jnp.clip(j - 4, 0, 1), 0, 0)),
        ],
        out_specs=[
            pl.BlockSpec((tm, D_MODEL), lambda i, j: (i, 0)),
            pl.BlockSpec((None, N_HEADS, tm, HEAD_DIM), lambda i, j: (jnp.clip(j - 1, 0, 2), 0, i, 0)),
            pl.BlockSpec((None, N_HEADS, tm, HEAD_DIM), lambda i, j: (jnp.clip(j - 4, 0, 1), 0, i, 0)),
            pl.BlockSpec((None, tm, D_MODEL), lambda i, j: (jnp.clip(j - 6, 0, 1), i, 0)),
        ],
        out_shape=[
            jax.ShapeDtypeStruct((n, D_MODEL), BF16),
            jax.ShapeDtypeStruct((3, N_HEADS, n, HEAD_DIM), BF16),
            jax.ShapeDtypeStruct((2, N_HEADS, n, HEAD_DIM), F32),
            jax.ShapeDtypeStruct((2, n, D_MODEL), BF16),
        ],
        compiler_params=_params(("parallel", "arbitrary"), VMEM_LIMIT), name="proj",
    )(h, w, w, lb)


def _conv_kernel(first_ref, last_ref, up_ref, u_ref, un_ref, w_ref, b_ref, g_ref, beta_ref, o_ref,
                 buf_ref, acc_ref, *, rows):
    i = pl.program_id(0)
    t = u_ref.shape[0]
    nslab = D_MODEL // 128
    prev = jnp.where(first_ref[i] == 1, 0.0, up_ref[...].astype(F32))
    nxt = jnp.where(last_ref[i] == 1, 0.0, un_ref[...].astype(F32))
    for c in range(nslab):
        cols = slice(c * 128, (c + 1) * 128)
        buf_ref[c, 0:HALO, :] = prev[:, cols]
        buf_ref[c, HALO:HALO + t, :] = u_ref[:, cols].astype(F32)
        buf_ref[c, HALO + t:2 * HALO + t, :] = nxt[:, cols]

    def slab(c, carry):
        for rb in range(t // rows):
            acc = jnp.zeros((rows, 128), F32)
            for k in range(CONV_WIDTH):
                r0 = rb * rows + HALO - CONV_PAD + k
                acc = acc + w_ref[c, k:k + 1, :] * buf_ref[c, r0:r0 + rows, :]
            acc_ref[c, rb * rows:(rb + 1) * rows, :] = acc
        return carry

    lax.fori_loop(0, nslab, slab, 0)
    x = jnp.concatenate([acc_ref[c] for c in range(nslab)], axis=-1) + b_ref[...]
    y = _layer_norm_rows(x, g_ref[...], beta_ref[...])
    o_ref[...] = (y * _sigmoid(y)).astype(BF16)


def _conv_call(u, first, last, w, b, g, beta, t):
    n = u.shape[0]
    nb, hb = n // t, t // HALO
    nslab = D_MODEL // 128
    wpad = jnp.zeros((32, D_MODEL), F32).at[:CONV_WIDTH].set(w).reshape(32, nslab, 128).transpose(1, 0, 2)
    vec = pl.BlockSpec((1, D_MODEL), lambda i, *_: (0, 0))
    return pl.pallas_call(
        functools.partial(_conv_kernel, rows=32),
        grid_spec=pltpu.PrefetchScalarGridSpec(
            num_scalar_prefetch=2, grid=(nb,),
            in_specs=[
                pl.BlockSpec((HALO, D_MODEL), lambda i, *_: (jnp.maximum(i * hb - 1, 0), 0)),
                pl.BlockSpec((t, D_MODEL), lambda i, *_: (i, 0)),
                pl.BlockSpec((HALO, D_MODEL), lambda i, *_: (jnp.minimum((i + 1) * hb, nb * hb - 1), 0)),
                pl.BlockSpec((nslab, 32, 128), lambda i, *_: (0, 0, 0)),
                vec, vec, vec,
            ],
            out_specs=pl.BlockSpec((t, D_MODEL), lambda i, *_: (i, 0)),
            scratch_shapes=[pltpu.VMEM((nslab, t + 2 * HALO, 128), F32), pltpu.VMEM((nslab, t, 128), F32)],
        ),
        out_shape=jax.ShapeDtypeStruct((n, D_MODEL), BF16),
        compiler_params=_params(("parallel",), VMEM_LIMIT), name="conv",
    )(first, last, u, u, u, wpad, b.reshape(1, -1), g.reshape(1, -1), beta.reshape(1, -1))


def _mul(a, b):
    if a is None:
        return b
    if b is None:
        return a
    return a * b


def _cat(tiles):
    return jnp.concatenate(tiles, axis=0)


def _nt(a, b):
    return lax.dot_general(a, b, (((1,), (1,)), ((), ())), preferred_element_type=F32)


def _tn(a, b):
    return lax.dot_general(a, b, (((0,), (0,)), ((), ())), preferred_element_type=F32)


def _gla_masks():
    i = lax.broadcasted_iota(jnp.int32, (CHUNK, CHUNK), 0)
    j = lax.broadcasted_iota(jnp.int32, (CHUNK, CHUNK), 1)
    rt, rs = i & 7, j & 7
    masks = []
    for lvl in range(3):
        masks.append((rt >> (lvl + 1)) == (rs >> (lvl + 1)))
    return masks


def _block_products(tot, rev):
    ri = lax.broadcasted_iota(jnp.int32, tot.shape, 0)
    r = (7 - ri) if rev else ri

    def sh(x, d):
        return pltpu.roll(x, ((-d) if rev else d) % 8, 0)

    one = jnp.ones_like(tot)
    zero = jnp.zeros_like(tot)
    b0, b1, b2 = (r & 1) == 1, (r & 2) == 2, (r & 4) == 4
    m4 = r & 3
    d1, u1 = sh(tot, 1), sh(tot, -1)

    def excl_prefix(width):
        m = r & (width - 1)
        e = jnp.where(m >= 1, d1, one)
        step = 1
        while step < width:
            e = e * jnp.where(m >= step, sh(e, step), one)
            step *= 2
        return e

    def excl_suffix(width):
        m = r & (width - 1)
        e = jnp.where(m <= width - 2, u1, one)
        step = 1
        while step < width:
            e = e * jnp.where(m <= width - 1 - step, sh(e, -step), one)
            step *= 2
        return e

    lvl3 = (jnp.where(b0, one, zero), jnp.where(b0, zero, one))
    lvl4 = (jnp.where(b1, jnp.where(b0, d1, one), zero), jnp.where(b1, zero, jnp.where(b0, one, u1)))
    lvl5 = (jnp.where(b2, excl_prefix(4), zero), jnp.where(b2, zero, excl_suffix(4)))
    state = (excl_prefix(8), excl_suffix(8))
    del m4
    return [lvl3, lvl4, lvl5, state]


def _gla_prep(fs, qs, vs, rev):
    n = 8
    ks = [1.0 - f for f in fs]
    fi = [list(fs)]
    fi.append([fs[a] * fs[a - 1] if a & 1 else fs[a] for a in range(n)])
    fi.append([fi[1][a] * fi[1][(a & ~3) + 1] if a & 2 else fi[1][a] for a in range(n)])
    fi.append([fi[2][a] * fi[2][3] if a & 4 else fi[2][a] for a in range(n)])
    ge = [[None] * n]
    ge.append([fs[a + 1] if not a & 1 else None for a in range(n)])
    ge.append([_mul(ge[1][a], fi[1][(a & ~3) + 3]) if not a & 2 else ge[1][a] for a in range(n)])
    ge.append([_mul(ge[2][a], fi[2][7]) if not a & 4 else ge[2][a] for a in range(n)])

    low = [jnp.sum(qs[a] * ks[a], axis=-1, keepdims=True) * vs[a] for a in range(n)]
    for lvl in range(3):
        half = 1 << lvl
        kt = [None if (a >> lvl) & 1 else _mul(ks[a], ge[lvl][a]) for a in range(n)]
        for a in range(n):
            if not (a >> lvl) & 1:
                continue
            qt = qs[a] * fi[lvl][a]
            first = a & ~(2 * half - 1)
            for s in range(first, first + half):
                low[a] = low[a] + jnp.sum(qt * kt[s], axis=-1, keepdims=True) * vs[s]
    low = _cat(low)

    levels = []
    qf = [qs[a] * fi[3][a] for a in range(n)]
    kg = [_mul(ks[a], ge[3][a]) for a in range(n)]
    tot = fi[3][7]
    facs = _block_products(tot, rev)
    for lvl in range(3):
        rq, rk = facs[lvl]
        levels.append((_cat([x * rq for x in qf]).astype(BF16), _cat([x * rk for x in kg]).astype(BF16)))
    rq, rk = facs[3]
    q_in = _cat([x * rq for x in qf]).astype(BF16)
    k_out = _cat([x * rk for x in kg]).astype(BF16)
    last = 0 if rev else 7
    decay = (rq * tot)[last:last + 1, :]
    return levels, (_cat(vs).astype(BF16), q_in, k_out, decay, low)


def _gla_kernel(flag_ref, q_ref, v_ref, f_ref, *rest, rev, final):
    if final:
        (obw_ref, og_ref, g_ref, o_ref, st_ref, qs_ref, vs_ref, ops_ref, dec_ref, dg_ref, att_ref, kv_ref,
         os_ref) = rest
    else:
        o_ref, st_ref, qs_ref, vs_ref, ops_ref, dec_ref, dg_ref, att_ref, kv_ref = rest
        os_ref = o_ref
    nb = pl.num_programs(1)
    tb = pl.program_id(1)
    blk = (nb - 1 - tb) if rev else tb
    t = q_ref.shape[0]
    nchunks = t // CHUNK

    @pl.when(flag_ref[blk] == 1)
    def _():
        st_ref[...] = jnp.zeros_like(st_ref)

    qs_ref[...] = q_ref[...].astype(F32)
    vs_ref[...] = v_ref[...].astype(F32)
    masks = _gla_masks()
    order = list(range(7, -1, -1)) if rev else list(range(8))

    def rows_of(ci):
        c0 = ((nchunks - 1 - ci) if rev else ci) * CHUNK
        if not isinstance(c0, int):
            c0 = pl.multiple_of(c0, CHUNK)
        return [pl.ds(c0 + a, 8, stride=8) for a in order]

    def prep(ci):
        rows = rows_of(ci)
        return _gla_prep([f_ref[rw, :] for rw in rows], [qs_ref[rw, :] for rw in rows],
                         [vs_ref[rw, :] for rw in rows], rev)

    grp = GLA_GROUP
    ngroups = nchunks // grp
    nlev = len(masks)

    def operands(g, slot):
        for u in range(grp):
            levels, (vb, q_in, k_out, decay, diag) = prep(g * grp + u)
            for k, (qt, kt) in enumerate(levels):
                ops_ref[slot, u, 2 * k] = qt
                ops_ref[slot, u, 2 * k + 1] = kt
            ops_ref[slot, u, 2 * nlev] = vb
            ops_ref[slot, u, 2 * nlev + 1] = q_in
            ops_ref[slot, u, 2 * nlev + 2] = k_out
            dec_ref[slot, u] = jnp.broadcast_to(decay, (8, HEAD_DIM))
            dg_ref[slot, u] = diag

    def products(slot):
        for u in range(grp):
            att = None
            for k, mask in enumerate(masks):
                part = jnp.where(mask, _nt(ops_ref[slot, u, 2 * k], ops_ref[slot, u, 2 * k + 1]), 0.0)
                att = part if att is None else att + part
            att_ref[slot, u] = att.astype(BF16)
            kv_ref[slot, u] = _tn(ops_ref[slot, u, 2 * nlev], ops_ref[slot, u, 2 * nlev + 2])

    def finish(g, slot, st):
        for u in range(grp):
            out = (jnp.dot(att_ref[slot, u], ops_ref[slot, u, 2 * nlev], preferred_element_type=F32)
                   + _nt(ops_ref[slot, u, 2 * nlev + 1], st.astype(BF16)) + dg_ref[slot, u])
            st = st * dec_ref[slot, u, 0:1, :] + kv_ref[slot, u]
            for a, rw in enumerate(rows_of(g * grp + u)):
                os_ref[rw, :] = out[8 * a:8 * a + 8]
        return st

    def body(g, st):
        slot = g % 2
        st = finish(g, slot, st)
        products(1 - slot)
        operands(g + 2, slot)
        return st

    operands(0, 0)
    products(0)
    operands(1, 1)
    st = lax.fori_loop(0, ngroups - 2, body, st_ref[...])
    st = finish(ngroups - 2, ngroups % 2, st)
    products(1 - ngroups % 2)
    st_ref[...] = finish(ngroups - 1, 1 - ngroups % 2, st)

    if final:
        o = os_ref[...] + obw_ref[...]
        o = o * lax.rsqrt(jnp.mean(o * o, axis=-1, keepdims=True) + RMS_EPS) * g_ref[...]
        o_ref[...] = (o * og_ref[...].astype(F32)).astype(BF16)


def _gla_call(flags, qvo, f, direction, t, obw=None, g=None):
    n = qvo.shape[2]
    nb = n // t
    rev = direction == 1
    final = not rev

    def tmap(hd, tb, *_):
        return (nb - 1 - tb) if rev else tb

    def sel(k):
        return pl.BlockSpec((None, None, t, HEAD_DIM), lambda hd, tb, *_: (k, hd, tmap(hd, tb), 0))

    in_specs = [sel(0), sel(1), sel(direction)]
    args = [qvo, qvo, f]
    scratch = [pltpu.VMEM((HEAD_DIM, HEAD_DIM), F32), pltpu.VMEM((t, HEAD_DIM), F32),
               pltpu.VMEM((t, HEAD_DIM), F32),
               pltpu.VMEM((2, GLA_GROUP, 9, CHUNK, HEAD_DIM), BF16),
               pltpu.VMEM((2, GLA_GROUP, 8, HEAD_DIM), F32),
               pltpu.VMEM((2, GLA_GROUP, CHUNK, HEAD_DIM), F32),
               pltpu.VMEM((2, GLA_GROUP, CHUNK, CHUNK), BF16),
               pltpu.VMEM((2, GLA_GROUP, HEAD_DIM, HEAD_DIM), F32)]
    if final:
        in_specs += [pl.BlockSpec((None, t, HEAD_DIM), lambda hd, tb, *_: (hd, tb, 0)), sel(2),
                     pl.BlockSpec((None, 1, HEAD_DIM), lambda hd, tb, *_: (hd, 0, 0))]
        args += [obw, qvo, g.reshape(N_HEADS, 1, HEAD_DIM)]
        scratch.append(pltpu.VMEM((t, HEAD_DIM), F32))
    return pl.pallas_call(
        functools.partial(_gla_kernel, rev=rev, final=final),
        grid_spec=pltpu.PrefetchScalarGridSpec(
            num_scalar_prefetch=1, grid=(N_HEADS, nb), in_specs=in_specs,
            out_specs=pl.BlockSpec((None, t, HEAD_DIM), lambda hd, tb, *_: (hd, tmap(hd, tb), 0)),
            scratch_shapes=scratch,
        ),
        out_shape=jax.ShapeDtypeStruct((N_HEADS, n, HEAD_DIM), BF16 if final else F32),
        compiler_params=_params(("parallel", "arbitrary"), VMEM_LIMIT),
        name="gla_fwd" if final else "gla_bwd",
    )(flags, *args)


def _merge_kernel(h_ref, ua_ref, ob_ref, gab_ref, wc_ref, whg_ref, wo_ref, g_ref, b_ref, o_ref):
    ob = jnp.concatenate([ob_ref[hd] for hd in range(N_HEADS)], axis=-1)
    branch_a = jnp.dot(ua_ref[...], wc_ref[...], preferred_element_type=F32)
    branch_b = jnp.dot(ob, whg_ref[...], preferred_element_type=F32)
    merged = gab_ref[0].astype(F32) * branch_a + gab_ref[1].astype(F32) * branch_b
    mix = jnp.dot(merged.astype(BF16), wo_ref[...], preferred_element_type=F32)
    o_ref[...] = _layer_norm_rows(ALPHA * h_ref[...] + mix, g_ref[...], b_ref[...])


def _merge_call(h, ua, ob, gab, wc, whg, wo, g, b, tm):
    n = h.shape[0]
    row = pl.BlockSpec((tm, D_MODEL), lambda i: (i, 0))
    mat = pl.BlockSpec((D_MODEL, D_MODEL), lambda i: (0, 0))
    vec = pl.BlockSpec((1, D_MODEL), lambda i: (0, 0))
    return pl.pallas_call(
        _merge_kernel, grid=(n // tm,),
        in_specs=[row, row, pl.BlockSpec((N_HEADS, tm, HEAD_DIM), lambda i: (0, i, 0)),
                  pl.BlockSpec((2, tm, D_MODEL), lambda i: (0, i, 0)), mat, mat, mat, vec, vec],
        out_specs=row, out_shape=jax.ShapeDtypeStruct((n, D_MODEL), F32),
        compiler_params=_params(("parallel",), VMEM_LIMIT), name="merge",
    )(h, ua, ob, gab, wc, whg, wo, g.reshape(1, -1), b.reshape(1, -1))


def _router_kernel(h_ref, w_ref, b_ref, cls_ref, gate_ref):
    logits = lax.dot_general(w_ref[...], h_ref[...].astype(BF16), (((1,), (1,)), ((), ())),
                             preferred_element_type=F32) + b_ref[...]
    rows = [logits[e:e + 1, :] for e in range(N_EXPERTS)]
    mx = functools.reduce(jnp.maximum, rows)
    ex = [jnp.exp(x - mx) for x in rows]
    den = functools.reduce(lambda a, b: a + b, ex)
    sc = [e / den for e in ex]
    gscore = []
    for grp in range(N_GROUPS):
        s = sc[grp * 4:grp * 4 + 4]
        gscore.append(functools.reduce(jnp.maximum, [s[a] + s[b] for a, b in PAIRS]))
    gsel = jnp.zeros_like(gscore[0], dtype=jnp.int32)
    best = gscore[0]
    for grp in range(1, N_GROUPS):
        better = gscore[grp] > best
        gsel = jnp.where(better, grp, gsel)
        best = jnp.where(better, gscore[grp], best)
    ing = []
    for k in range(EXPERTS_PER_GROUP):
        x = sc[k]
        for grp in range(1, N_GROUPS):
            x = jnp.where(gsel == grp, sc[grp * 4 + k], x)
        ing.append(x)
    i1 = jnp.zeros_like(gsel)
    w1 = ing[0]
    for k in range(1, 4):
        better = ing[k] > w1
        i1 = jnp.where(better, k, i1)
        w1 = jnp.where(better, ing[k], w1)
    i2 = jnp.full_like(gsel, -1)
    w2 = jnp.full_like(w1, -1.0)
    for k in range(4):
        better = (i1 != k) & (ing[k] > w2)
        i2 = jnp.where(better, k, i2)
        w2 = jnp.where(better, ing[k], w2)
    tot = w1 + w2
    g1, g2 = w1 / tot, w2 / tot
    lo, hi = jnp.minimum(i1, i2), jnp.maximum(i1, i2)
    pair = jnp.where(lo == 0, hi - 1, jnp.where(lo == 1, hi + 1, 5))
    cls_ref[...] = gsel * len(PAIRS) + pair
    first_is_lo = i1 < i2
    gate_ref[0:1, :] = jnp.where(first_is_lo, g1, g2)
    gate_ref[1:2, :] = jnp.where(first_is_lo, g2, g1)


def _router_call(h, w_t, b, tm):
    n = h.shape[0]
    return pl.pallas_call(
        _router_kernel, grid=(n // tm,),
        in_specs=[pl.BlockSpec((tm, D_MODEL), lambda i: (i, 0)),
                  pl.BlockSpec((N_EXPERTS, D_MODEL), lambda i: (0, 0)),
                  pl.BlockSpec((N_EXPERTS, 1), lambda i: (0, 0))],
        out_specs=[pl.BlockSpec((1, tm), lambda i: (0, i)), pl.BlockSpec((2, tm), lambda i: (0, i))],
        out_shape=[jax.ShapeDtypeStruct((1, n), jnp.int32), jax.ShapeDtypeStruct((2, n), F32)],
        compiler_params=_params(("parallel",)), name="router",
    )(h, w_t, b)


def _gather_kernel(idx_ref, x_ref, o_ref, sem):
    tm = o_ref.shape[0]
    base = pl.program_id(0) * tm

    def start(r, carry):
        pltpu.make_async_copy(x_ref.at[pl.ds(idx_ref[base + r], 1), :], o_ref.at[pl.ds(r, 1), :], sem).start()
        return carry

    lax.fori_loop(0, tm, start, 0, unroll=8)
    pltpu.make_async_copy(x_ref.at[pl.ds(0, tm), :], o_ref, sem).wait()


def _gather_call(x, idx, tm):
    n_out = idx.shape[0]
    return pl.pallas_call(
        _gather_kernel,
        grid_spec=pltpu.PrefetchScalarGridSpec(
            num_scalar_prefetch=1, grid=(n_out // tm,),
            in_specs=[pl.BlockSpec(memory_space=pl.ANY)],
            out_specs=pl.BlockSpec((tm, D_MODEL), lambda i, *_: (i, 0)),
            scratch_shapes=[pltpu.SemaphoreType.DMA(())],
        ),
        out_shape=jax.ShapeDtypeStruct((n_out, D_MODEL), x.dtype),
        compiler_params=_params(("arbitrary",)), name="gather",
    )(idx, x)


def _gelu_tanh(x):
    return 0.5 * x * (1.0 + jnp.tanh(0.7978845608028654 * (x + 0.044715 * x * x * x)))


def _moe_kernel(ea_ref, eb_ref, used_ref, x_ref, g_ref, w1a_ref, w2a_ref, w1b_ref, w2b_ref, o_ref):
    @pl.when(pl.program_id(0) < used_ref[0])
    def _():
        x = x_ref[...].astype(BF16)

        def expert(w1_ref, w2_ref):
            hid = _gelu_tanh(jnp.dot(x, w1_ref[...], preferred_element_type=F32))
            return jnp.dot(hid.astype(BF16), w2_ref[...], preferred_element_type=F32)

        g = g_ref[...]
        o_ref[...] = g[:, 0:1] * expert(w1a_ref, w2a_ref) + g[:, 1:2] * expert(w1b_ref, w2b_ref)

    @pl.when(pl.program_id(0) >= used_ref[0])
    def _():
        o_ref[...] = jnp.zeros_like(o_ref)


def _moe_call(xs, gates, blk_ea, blk_eb, n_used, w1, w2, tm):
    npad = xs.shape[0]
    w1a = pl.BlockSpec((None, D_MODEL, D_EXPERT), lambda i, ea, eb, u: (ea[i], 0, 0))
    w2a = pl.BlockSpec((None, D_EXPERT, D_MODEL), lambda i, ea, eb, u: (ea[i], 0, 0))
    w1b = pl.BlockSpec((None, D_MODEL, D_EXPERT), lambda i, ea, eb, u: (eb[i], 0, 0))
    w2b = pl.BlockSpec((None, D_EXPERT, D_MODEL), lambda i, ea, eb, u: (eb[i], 0, 0))
    return pl.pallas_call(
        _moe_kernel,
        grid_spec=pltpu.PrefetchScalarGridSpec(
            num_scalar_prefetch=3, grid=(npad // tm,),
            in_specs=[pl.BlockSpec((tm, D_MODEL), lambda i, *_: (i, 0)),
                      pl.BlockSpec((tm, 2), lambda i, *_: (i, 0)), w1a, w2a, w1b, w2b],
            out_specs=pl.BlockSpec((tm, D_MODEL), lambda i, *_: (i, 0)),
        ),
        out_shape=jax.ShapeDtypeStruct((npad, D_MODEL), F32),
        compiler_params=_params(("arbitrary",), VMEM_LIMIT), name="moe",
    )(blk_ea, blk_eb, n_used, xs, gates, w1, w2, w1, w2)


def _routing_tables(cls, gates, tm):
    n = cls.shape[0]
    nblk = n // tm + N_CLASSES
    counts = jnp.zeros((N_CLASSES,), jnp.int32).at[cls].add(1)
    starts = jnp.cumsum(counts) - counts
    padded = (counts + tm - 1) // tm * tm
    pad_end = jnp.cumsum(padded)
    order = jnp.argsort(cls, stable=True).astype(jnp.int32)
    c_sorted = cls[order]
    dest_sorted = pad_end[c_sorted] - padded[c_sorted] + jnp.arange(n, dtype=jnp.int32) - starts[c_sorted]
    src = jnp.zeros((nblk * tm,), jnp.int32).at[dest_sorted].set(order)
    dest = jnp.zeros((n,), jnp.int32).at[order].set(dest_sorted)
    gates_sorted = jnp.zeros((nblk * tm, 2), F32).at[dest].set(gates)
    n_used = pad_end[-1] // tm
    blk = jnp.arange(nblk, dtype=jnp.int32)
    blk_cls = jnp.searchsorted(pad_end, jnp.minimum(blk, n_used - 1) * tm, side="right").astype(jnp.int32)
    blk_cls = jnp.minimum(blk_cls, N_CLASSES - 1)
    pairs = jnp.asarray(PAIRS, jnp.int32)
    grp, pr = blk_cls // len(PAIRS), blk_cls % len(PAIRS)
    blk_ea = grp * EXPERTS_PER_GROUP + pairs[pr, 0]
    blk_eb = grp * EXPERTS_PER_GROUP + pairs[pr, 1]
    return src, dest, gates_sorted, blk_ea, blk_eb, n_used.reshape(1).astype(jnp.int32)


def _block_flags(seq_lens, t):
    first, last = [], []
    for length in seq_lens:
        nb = length // t
        first += [1] + [0] * (nb - 1)
        last += [0] * (nb - 1) + [1]
    return jnp.asarray(np.array(first, np.int32)), jnp.asarray(np.array(last, np.int32))


def _trunk(x, seq_lens, p, *, tm, t_conv, t_gla, tm_moe):
    (ln_in_g, ln_in_b, w_in, dw_w, dw_b, conv_ln_g, conv_ln_b, w_conv_out, hg_lower, hg_norm_g, w_hg_out,
     w_o, ln1_g, ln1_b, w_router, b_router, w_e1, w_e2, ln2_g, ln2_b) = p
    depth = w_in.shape[0]
    conv_first, conv_last = _block_flags(seq_lens, t_conv)
    gla_first, gla_last = _block_flags(seq_lens, t_gla)

    lb_p = jax.nn.softmax(hg_lower.astype(F32), axis=0)
    lb_all = jnp.clip(jnp.cumsum(lb_p, axis=0) - lb_p[0:1], 0.0, 1.0)
    w_in_g = w_in.reshape(depth, D_MODEL, 9, D_MODEL)[:, :, jnp.asarray([0, 1, 2, 5, 6, 3, 4, 7, 8])]
    w_in_b = w_in_g.reshape(depth, D_MODEL, 9 * D_MODEL).astype(BF16)
    w_router_t = w_router.T.astype(BF16)
    b_router_c = b_router.reshape(N_EXPERTS, 1).astype(F32)

    h = _ln_call(x, ln_in_g, ln_in_b, tm)
    for l in range(depth):
        u, qvo, f, gab = _proj_call(h, w_in_b[l], lb_all[l].reshape(2, 1, D_MODEL), tm)
        ua = _conv_call(u, conv_first, conv_last, dw_w[l], dw_b[l], conv_ln_g[l], conv_ln_b[l], t_conv)
        o_bw = _gla_call(gla_last, qvo, f, 1, t_gla)
        ob = _gla_call(gla_first, qvo, f, 0, t_gla, obw=o_bw, g=hg_norm_g[l].astype(F32))
        h1 = _merge_call(h, ua, ob, gab, w_conv_out[l].astype(BF16), w_hg_out[l].astype(BF16),
                         w_o[l].astype(BF16), ln1_g[l], ln1_b[l], tm)
        cls, gates = _router_call(h1, w_router_t, b_router_c, tm)
        src, dest, gates_sorted, blk_ea, blk_eb, n_used = _routing_tables(cls[0], gates.T, tm_moe)
        xs = _gather_call(h1, src, tm_moe)
        ys = _moe_call(xs, gates_sorted, blk_ea, blk_eb, n_used, w_e1[l].astype(BF16), w_e2[l].astype(BF16),
                       tm_moe)
        y = _gather_call(ys, dest, tm_moe)
        h = _res_ln_call(h1, y, ln2_g[l], ln2_b[l], tm)
    return h


def kernel(x_prompt, x_sample, ln_in_g, ln_in_b, w_in, dw_w, dw_b, conv_ln_g, conv_ln_b, w_conv_out, hg_lower,
           hg_norm_g, w_hg_out, w_o, ln1_g, ln1_b, w_router, b_router, w_e1, w_e2, ln2_g, ln2_b):
    bp, lp, d = x_prompt.shape
    bs, ls, _ = x_sample.shape
    x = jnp.concatenate([x_prompt.reshape(bp * lp, d), x_sample.reshape(bs * ls, d)], axis=0)
    seq_lens = (lp,) * bp + (ls,) * bs
    p = (ln_in_g, ln_in_b, w_in, dw_w, dw_b, conv_ln_g, conv_ln_b, w_conv_out, hg_lower, hg_norm_g, w_hg_out,
         w_o, ln1_g, ln1_b, w_router, b_router, w_e1, w_e2, ln2_g, ln2_b)
    h = _trunk(x, seq_lens, p, tm=512, t_conv=512, t_gla=2048, tm_moe=256)
    return (h[:bp * lp].reshape(bp, lp, d), h[bp * lp:].reshape(bs, ls, d))
```

```python
import functools
import math

import numpy as np
import jax
import jax.numpy as jnp
from jax import lax
from jax.experimental import pallas as pl
from jax.experimental.pallas import tpu as pltpu

F32 = jnp.float32
BF16 = jnp.bfloat16

D_MODEL = 1024
DEPTH = 4
HEAD_DIM = 128
N_HEADS = D_MODEL // HEAD_DIM
CONV_WIDTH = 31
CONV_PAD = CONV_WIDTH // 2
HALO = 16
CHUNK = 64
GLA_GROUP = 16
EXP_CLAMP = 60.0
N_EXPERTS = 16
N_GROUPS = 4
EXPERTS_PER_GROUP = 4
D_EXPERT = 2 * D_MODEL
PAIRS = ((0, 1), (0, 2), (0, 3), (1, 2), (1, 3), (2, 3))
N_CLASSES = N_GROUPS * len(PAIRS)
ALPHA = (2 * DEPTH) ** 0.25
LN_EPS = 1e-5
RMS_EPS = 1e-6
VMEM_LIMIT = 56 * 1024 * 1024


def _params(sem, vmem=None):
    return pltpu.CompilerParams(dimension_semantics=sem, vmem_limit_bytes=vmem)


def _sigmoid(x):
    return 1.0 / (1.0 + jnp.exp(-x))


def _layer_norm_rows(x, g, b):
    mu = jnp.mean(x, axis=-1, keepdims=True)
    xc = x - mu
    var = jnp.mean(xc * xc, axis=-1, keepdims=True)
    return xc * lax.rsqrt(var + LN_EPS) * g + b


def _ln_kernel(xa_ref, xb_ref, g_ref, b_ref, o_ref, *, split):
    i = pl.program_id(0)

    @pl.when(i < split)
    def _():
        o_ref[...] = _layer_norm_rows(xa_ref[...], g_ref[...], b_ref[...])

    @pl.when(i >= split)
    def _():
        o_ref[...] = _layer_norm_rows(xb_ref[...], g_ref[...], b_ref[...])


def _ln_call(xa, xb, g, b, tm):
    na, nb = xa.shape[0], xb.shape[0]
    split = na // tm
    vec = pl.BlockSpec((1, D_MODEL), lambda i: (0, 0))
    return pl.pallas_call(
        functools.partial(_ln_kernel, split=split), grid=((na + nb) // tm,),
        in_specs=[pl.BlockSpec((tm, D_MODEL), lambda i: (jnp.minimum(i, split - 1), 0)),
                  pl.BlockSpec((tm, D_MODEL), lambda i: (jnp.maximum(i - split, 0), 0)), vec, vec],
        out_specs=pl.BlockSpec((tm, D_MODEL), lambda i: (i, 0)),
        out_shape=jax.ShapeDtypeStruct((na + nb, D_MODEL), F32),
        compiler_params=_params(("arbitrary",)), name="ln_in",
    )(xa, xb, g.reshape(1, -1), b.reshape(1, -1))


PROJ_COLS = 256


def _proj_kernel(x_ref, wv_ref, w_ref, lb_ref, u_ref, qvo_ref, f_ref, gab_ref):
    j = pl.program_id(1)
    x = x_ref[...].astype(BF16)
    pieces = [slice(c * PROJ_COLS, (c + 1) * PROJ_COLS) for c in range(D_MODEL // PROJ_COLS)]

    def mm(ref, cols):
        return jnp.dot(x, ref[:, cols], preferred_element_type=F32)

    def heads_store(ref, cols, val):
        for k in range(PROJ_COLS // HEAD_DIM):
            ref[cols.start // HEAD_DIM + k] = val[:, k * HEAD_DIM:(k + 1) * HEAD_DIM].astype(ref.dtype)

    @pl.when(j == 0)
    def _():
        for cols in pieces:
            u_ref[:, cols] = (mm(wv_ref, cols) * _sigmoid(mm(w_ref, cols))).astype(BF16)

    @pl.when(j == 1)
    def _():
        for cols in pieces:
            a = mm(w_ref, cols)
            heads_store(qvo_ref, cols, a * _sigmoid(a) * HEAD_DIM ** -0.5)

    @pl.when(j == 2)
    def _():
        for cols in pieces:
            heads_store(qvo_ref, cols, mm(w_ref, cols))

    @pl.when(j == 3)
    def _():
        for cols in pieces:
            a = mm(w_ref, cols)
            heads_store(qvo_ref, cols, a * _sigmoid(a))

    @pl.when((j == 4) | (j == 5))
    def _():
        for cols in pieces:
            t = jnp.exp(-mm(w_ref, cols))
            f = (1.0 + lb_ref[:, cols] * jnp.minimum(t, math.exp(EXP_CLAMP))) / (1.0 + t)
            heads_store(f_ref, cols, jnp.minimum(f, 1.0))

    @pl.when(j >= 6)
    def _():
        for cols in pieces:
            gab_ref[:, cols] = _sigmoid(mm(w_ref, cols)).astype(BF16)


def _proj_call(h, w, lb, tm):
    n = h.shape[0]
    return pl.pallas_call(
        _proj_kernel, grid=(n // tm, 8),
        in_specs=[
            pl.BlockSpec((tm, D_MODEL), lambda i, j: (i, 0)),
            pl.BlockSpec((D_MODEL, D_MODEL), lambda i, j: (0, 0)),
            pl.BlockSpec((D_MODEL, D_MODEL), lambda i, j: (0, j + 1)),
            pl.BlockSpec((None, 1, D_MODEL), lambda i, j: (jnp.clip(j - 4, 0, 1), 0, 0)),
        ],
        out_specs=[
            pl.BlockSpec((tm, D_MODEL), lambda i, j: (i, 0)),
            pl.BlockSpec((None, N_HEADS, tm, HEAD_DIM), lambda i, j: (jnp.clip(j - 1, 0, 2), 0, i, 0)),
            pl.BlockSpec((None, N_HEADS, tm, HEAD_DIM), lambda i, j: (jnp.clip(j - 4, 0, 1), 0, i, 0)),
            pl.BlockSpec((None, tm, D_MODEL), lambda i, j: (jnp.clip(j - 6, 0, 1), i, 0)),
        ],
        out_shape=[
            jax.ShapeDtypeStruct((n, D_MODEL), BF16),
            jax.ShapeDtypeStruct((3, N_HEADS, n, HEAD_DIM), BF16),
            jax.ShapeDtypeStruct((2, N_HEADS, n, HEAD_DIM), F32),
            jax.ShapeDtypeStruct((2, n, D_MODEL), BF16),
        ],
        compiler_params=_params(("parallel", "arbitrary"), VMEM_LIMIT), name="proj",
    )(h, w, w, lb)


def _conv_kernel(first_ref, last_ref, up_ref, u_ref, un_ref, w_ref, b_ref, g_ref, beta_ref, o_ref,
                 buf_ref, acc_ref, *, rows):
    i = pl.program_id(0)
    t = u_ref.shape[0]
    nslab = D_MODEL // 128
    prev = jnp.where(first_ref[i] == 1, 0.0, up_ref[...].astype(F32))
    nxt = jnp.where(last_ref[i] == 1, 0.0, un_ref[...].astype(F32))
    for c in range(nslab):
        cols = slice(c * 128, (c + 1) * 128)
        buf_ref[c, 0:HALO, :] = prev[:, cols]
        buf_ref[c, HALO:HALO + t, :] = u_ref[:, cols].astype(F32)
        buf_ref[c, HALO + t:2 * HALO + t, :] = nxt[:, cols]

    def slab(c, carry):
        for rb in range(t // rows):
            acc = jnp.zeros((rows, 128), F32)
            for k in range(CONV_WIDTH):
                r0 = rb * rows + HALO - CONV_PAD + k
                acc = acc + w_ref[c, k:k + 1, :] * buf_ref[c, r0:r0 + rows, :]
            acc_ref[c, rb * rows:(rb + 1) * rows, :] = acc
        return carry

    lax.fori_loop(0, nslab, slab, 0)
    x = jnp.concatenate([acc_ref[c] for c in range(nslab)], axis=-1) + b_ref[...]
    y = _layer_norm_rows(x, g_ref[...], beta_ref[...])
    o_ref[...] = (y * _sigmoid(y)).astype(BF16)


def _conv_call(u, first, last, w, b, g, beta, t):
    n = u.shape[0]
    nb, hb = n // t, t // HALO
    nslab = D_MODEL // 128
    wpad = jnp.zeros((32, D_MODEL), F32).at[:CONV_WIDTH].set(w).reshape(32, nslab, 128).transpose(1, 0, 2)
    vec = pl.BlockSpec((1, D_MODEL), lambda i, *_: (0, 0))
    return pl.pallas_call(
        functools.partial(_conv_kernel, rows=32),
        grid_spec=pltpu.PrefetchScalarGridSpec(
            num_scalar_prefetch=2, grid=(nb,),
            in_specs=[
                pl.BlockSpec((HALO, D_MODEL), lambda i, *_: (jnp.maximum(i * hb - 1, 0), 0)),
                pl.BlockSpec((t, D_MODEL), lambda i, *_: (i, 0)),
                pl.BlockSpec((HALO, D_MODEL), lambda i, *_: (jnp.minimum((i + 1) * hb, nb * hb - 1), 0)),
                pl.BlockSpec((nslab, 32, 128), lambda i, *_: (0, 0, 0)),
                vec, vec, vec,
            ],
            out_specs=pl.BlockSpec((t, D_MODEL), lambda i, *_: (i, 0)),
            scratch_shapes=[pltpu.VMEM((nslab, t + 2 * HALO, 128), F32), pltpu.VMEM((nslab, t, 128), F32)],
        ),
        out_shape=jax.ShapeDtypeStruct((n, D_MODEL), BF16),
        compiler_params=_params(("parallel",), VMEM_LIMIT), name="conv",
    )(first, last, u, u, u, wpad, b.reshape(1, -1), g.reshape(1, -1), beta.reshape(1, -1))


def _mul(a, b):
    if a is None:
        return b
    if b is None:
        return a
    return a * b


def _cat(tiles):
    return jnp.concatenate(tiles, axis=0)


def _nt(a, b):
    return lax.dot_general(a, b, (((1,), (1,)), ((), ())), preferred_element_type=F32)


def _tn(a, b):
    return lax.dot_general(a, b, (((0,), (0,)), ((), ())), preferred_element_type=F32)


def _gla_masks():
    i = lax.broadcasted_iota(jnp.int32, (CHUNK, CHUNK), 0)
    j = lax.broadcasted_iota(jnp.int32, (CHUNK, CHUNK), 1)
    rt, rs = i & 7, j & 7
    masks = []
    for lvl in range(3):
        masks.append((rt >> (lvl + 1)) == (rs >> (lvl + 1)))
    return masks


def _block_products(tot, rev):
    ri = lax.broadcasted_iota(jnp.int32, tot.shape, 0)
    r = (7 - ri) if rev else ri

    def sh(x, d):
        return pltpu.roll(x, ((-d) if rev else d) % 8, 0)

    one = jnp.ones_like(tot)
    zero = jnp.zeros_like(tot)
    b0, b1, b2 = (r & 1) == 1, (r & 2) == 2, (r & 4) == 4
    m4 = r & 3
    d1, u1 = sh(tot, 1), sh(tot, -1)

    def excl_prefix(width):
        m = r & (width - 1)
        e = jnp.where(m >= 1, d1, one)
        step = 1
        while step < width:
            e = e * jnp.where(m >= step, sh(e, step), one)
            step *= 2
        return e

    def excl_suffix(width):
        m = r & (width - 1)
        e = jnp.where(m <= width - 2, u1, one)
        step = 1
        while step < width:
            e = e * jnp.where(m <= width - 1 - step, sh(e, -step), one)
            step *= 2
        return e

    lvl3 = (jnp.where(b0, one, zero), jnp.where(b0, zero, one))
    lvl4 = (jnp.where(b1, jnp.where(b0, d1, one), zero), jnp.where(b1, zero, jnp.where(b0, one, u1)))
    lvl5 = (jnp.where(b2, excl_prefix(4), zero), jnp.where(b2, zero, excl_suffix(4)))
    state = (excl_prefix(8), excl_suffix(8))
    del m4
    return [lvl3, lvl4, lvl5, state]


def _gla_prep(fs, qs, vs, rev):
    n = 8
    ks = [1.0 - f for f in fs]
    fi = [list(fs)]
    fi.append([fs[a] * fs[a - 1] if a & 1 else fs[a] for a in range(n)])
    fi.append([fi[1][a] * fi[1][(a & ~3) + 1] if a & 2 else fi[1][a] for a in range(n)])
    fi.append([fi[2][a] * fi[2][3] if a & 4 else fi[2][a] for a in range(n)])
    ge = [[None] * n]
    ge.append([fs[a + 1] if not a & 1 else None for a in range(n)])
    ge.append([_mul(ge[1][a], fi[1][(a & ~3) + 3]) if not a & 2 else ge[1][a] for a in range(n)])
    ge.append([_mul(ge[2][a], fi[2][7]) if not a & 4 else ge[2][a] for a in range(n)])

    low = [jnp.sum(qs[a] * ks[a], axis=-1, keepdims=True) * vs[a] for a in range(n)]
    for lvl in range(3):
        half = 1 << lvl
        kt = [None if (a >> lvl) & 1 else _mul(ks[a], ge[lvl][a]) for a in range(n)]
        for a in range(n):
            if not (a >> lvl) & 1:
                continue
            qt = qs[a] * fi[lvl][a]
            first = a & ~(2 * half - 1)
            for s in range(first, first + half):
                low[a] = low[a] + jnp.sum(qt * kt[s], axis=-1, keepdims=True) * vs[s]
    low = _cat(low)

    levels = []
    qf = [qs[a] * fi[3][a] for a in range(n)]
    kg = [_mul(ks[a], ge[3][a]) for a in range(n)]
    tot = fi[3][7]
    facs = _block_products(tot, rev)
    for lvl in range(3):
        rq, rk = facs[lvl]
        levels.append((_cat([x * rq for x in qf]).astype(BF16), _cat([x * rk for x in kg]).astype(BF16)))
    rq, rk = facs[3]
    q_in = _cat([x * rq for x in qf]).astype(BF16)
    k_out = _cat([x * rk for x in kg]).astype(BF16)
    last = 0 if rev else 7
    decay = (rq * tot)[last:last + 1, :]
    return levels, (_cat(vs).astype(BF16), q_in, k_out, decay, low)


def _gla_kernel(flag_ref, q_ref, v_ref, f_ref, *rest, rev, final):
    if final:
        (obw_ref, og_ref, g_ref, o_ref, st_ref, qs_ref, vs_ref, ops_ref, dec_ref, dg_ref, att_ref, kv_ref,
         os_ref) = rest
    else:
        o_ref, st_ref, qs_ref, vs_ref, ops_ref, dec_ref, dg_ref, att_ref, kv_ref = rest
        os_ref = o_ref
    nb = pl.num_programs(1)
    tb = pl.program_id(1)
    blk = (nb - 1 - tb) if rev else tb
    t = q_ref.shape[0]
    nchunks = t // CHUNK

    @pl.when(flag_ref[blk] == 1)
    def _():
        st_ref[...] = jnp.zeros_like(st_ref)

    qs_ref[...] = q_ref[...].astype(F32)
    vs_ref[...] = v_ref[...].astype(F32)
    masks = _gla_masks()
    order = list(range(7, -1, -1)) if rev else list(range(8))

    def rows_of(ci):
        c0 = ((nchunks - 1 - ci) if rev else ci) * CHUNK
        if not isinstance(c0, int):
            c0 = pl.multiple_of(c0, CHUNK)
        return [pl.ds(c0 + a, 8, stride=8) for a in order]

    def prep(ci):
        rows = rows_of(ci)
        return _gla_prep([f_ref[rw, :] for rw in rows], [qs_ref[rw, :] for rw in rows],
                         [vs_ref[rw, :] for rw in rows], rev)

    grp = ops_ref.shape[1]
    ngroups = nchunks // grp
    nlev = len(masks)

    def operands(g, slot):
        for u in range(grp):
            levels, (vb, q_in, k_out, decay, diag) = prep(g * grp + u)
            for k, (qt, kt) in enumerate(levels):
                ops_ref[slot, u, 2 * k] = qt
                ops_ref[slot, u, 2 * k + 1] = kt
            ops_ref[slot, u, 2 * nlev] = vb
            ops_ref[slot, u, 2 * nlev + 1] = q_in
            ops_ref[slot, u, 2 * nlev + 2] = k_out
            dec_ref[slot, u] = jnp.broadcast_to(decay, (8, HEAD_DIM))
            dg_ref[slot, u] = diag

    def products(slot):
        for u in range(grp):
            att = None
            for k, mask in enumerate(masks):
                part = jnp.where(mask, _nt(ops_ref[slot, u, 2 * k], ops_ref[slot, u, 2 * k + 1]), 0.0)
                att = part if att is None else att + part
            att_ref[slot, u] = att.astype(BF16)
            kv_ref[slot, u] = _tn(ops_ref[slot, u, 2 * nlev], ops_ref[slot, u, 2 * nlev + 2])

    def finish(g, slot, st):
        for u in range(grp):
            out = (jnp.dot(att_ref[slot, u], ops_ref[slot, u, 2 * nlev], preferred_element_type=F32)
                   + _nt(ops_ref[slot, u, 2 * nlev + 1], st.astype(BF16)) + dg_ref[slot, u])
            st = st * dec_ref[slot, u, 0:1, :] + kv_ref[slot, u]
            for a, rw in enumerate(rows_of(g * grp + u)):
                os_ref[rw, :] = out[8 * a:8 * a + 8]
        return st

    def body(g, st):
        slot = g % 2
        st = finish(g, slot, st)
        products(1 - slot)
        operands(g + 2, slot)
        return st

    operands(0, 0)
    products(0)
    operands(1, 1)
    st = lax.fori_loop(0, ngroups - 2, body, st_ref[...])
    st = finish(ngroups - 2, ngroups % 2, st)
    products(1 - ngroups % 2)
    st_ref[...] = finish(ngroups - 1, 1 - ngroups % 2, st)

    if final:
        o = os_ref[...] + obw_ref[...]
        o = o * lax.rsqrt(jnp.mean(o * o, axis=-1, keepdims=True) + RMS_EPS) * g_ref[...]
        o_ref[...] = (o * og_ref[...].astype(F32)).astype(BF16)


def _gla_call(flags, qvo, f, direction, t, obw=None, g=None):
    n = qvo.shape[2]
    nb = n // t
    rev = direction == 1
    final = not rev

    def tmap(hd, tb, *_):
        return (nb - 1 - tb) if rev else tb

    def sel(k):
        return pl.BlockSpec((None, None, t, HEAD_DIM), lambda hd, tb, *_: (k, hd, tmap(hd, tb), 0))

    in_specs = [sel(0), sel(1), sel(direction)]
    args = [qvo, qvo, f]
    grp = min(GLA_GROUP, t // CHUNK // 2)
    scratch = [pltpu.VMEM((HEAD_DIM, HEAD_DIM), F32), pltpu.VMEM((t, HEAD_DIM), F32),
               pltpu.VMEM((t, HEAD_DIM), F32),
               pltpu.VMEM((2, grp, 9, CHUNK, HEAD_DIM), BF16),
               pltpu.VMEM((2, grp, 8, HEAD_DIM), F32),
               pltpu.VMEM((2, grp, CHUNK, HEAD_DIM), F32),
               pltpu.VMEM((2, grp, CHUNK, CHUNK), BF16),
               pltpu.VMEM((2, grp, HEAD_DIM, HEAD_DIM), F32)]
    if final:
        in_specs += [pl.BlockSpec((None, t, HEAD_DIM), lambda hd, tb, *_: (hd, tb, 0)), sel(2),
                     pl.BlockSpec((None, 1, HEAD_DIM), lambda hd, tb, *_: (hd, 0, 0))]
        args += [obw, qvo, g.reshape(N_HEADS, 1, HEAD_DIM)]
        scratch.append(pltpu.VMEM((t, HEAD_DIM), F32))
    return pl.pallas_call(
        functools.partial(_gla_kernel, rev=rev, final=final),
        grid_spec=pltpu.PrefetchScalarGridSpec(
            num_scalar_prefetch=1, grid=(N_HEADS, nb), in_specs=in_specs,
            out_specs=pl.BlockSpec((None, t, HEAD_DIM), lambda hd, tb, *_: (hd, tmap(hd, tb), 0)),
            scratch_shapes=scratch,
        ),
        out_shape=jax.ShapeDtypeStruct((N_HEADS, n, HEAD_DIM), BF16 if final else F32),
        compiler_params=_params(("parallel", "arbitrary"), VMEM_LIMIT),
        name="gla_fwd" if final else "gla_bwd",
    )(flags, *args)


def _merge_kernel(h_ref, ua_ref, ob_ref, gab_ref, wc_ref, whg_ref, wo_ref, g_ref, b_ref, o_ref):
    ob = jnp.concatenate([ob_ref[hd] for hd in range(N_HEADS)], axis=-1)
    branch_a = jnp.dot(ua_ref[...], wc_ref[...], preferred_element_type=F32)
    branch_b = jnp.dot(ob, whg_ref[...], preferred_element_type=F32)
    merged = gab_ref[0].astype(F32) * branch_a + gab_ref[1].astype(F32) * branch_b
    mix = jnp.dot(merged.astype(BF16), wo_ref[...], preferred_element_type=F32)
    o_ref[...] = _layer_norm_rows(ALPHA * h_ref[...] + mix, g_ref[...], b_ref[...])


def _merge_call(h, ua, ob, gab, wc, whg, wo, g, b, tm):
    n = h.shape[0]
    row = pl.BlockSpec((tm, D_MODEL), lambda i: (i, 0))
    mat = pl.BlockSpec((D_MODEL, D_MODEL), lambda i: (0, 0))
    vec = pl.BlockSpec((1, D_MODEL), lambda i: (0, 0))
    return pl.pallas_call(
        _merge_kernel, grid=(n // tm,),
        in_specs=[row, row, pl.BlockSpec((N_HEADS, tm, HEAD_DIM), lambda i: (0, i, 0)),
                  pl.BlockSpec((2, tm, D_MODEL), lambda i: (0, i, 0)), mat, mat, mat, vec, vec],
        out_specs=row, out_shape=jax.ShapeDtypeStruct((n, D_MODEL), F32),
        compiler_params=_params(("parallel",), VMEM_LIMIT), name="merge",
    )(h, ua, ob, gab, wc, whg, wo, g.reshape(1, -1), b.reshape(1, -1))


GATE_LANES = 128
CLASS_ROWS = 32


def _router_kernel(h_ref, w_ref, b_ref, cls_ref, rank_ref, gate_ref, count_ref):
    @pl.when(pl.program_id(0) == 0)
    def _():
        count_ref[...] = jnp.zeros_like(count_ref)

    logits = lax.dot_general(w_ref[...], h_ref[...].astype(BF16), (((1,), (1,)), ((), ())),
                             preferred_element_type=F32) + b_ref[...]
    rows = [logits[e:e + 1, :] for e in range(N_EXPERTS)]
    mx = functools.reduce(jnp.maximum, rows)
    ex = [jnp.exp(x - mx) for x in rows]
    den = functools.reduce(lambda a, b: a + b, ex)
    sc = [e / den for e in ex]
    gscore = []
    for grp in range(N_GROUPS):
        s = sc[grp * 4:grp * 4 + 4]
        gscore.append(functools.reduce(jnp.maximum, [s[a] + s[b] for a, b in PAIRS]))
    gsel = jnp.zeros_like(gscore[0], dtype=jnp.int32)
    best = gscore[0]
    for grp in range(1, N_GROUPS):
        better = gscore[grp] > best
        gsel = jnp.where(better, grp, gsel)
        best = jnp.where(better, gscore[grp], best)
    ing = []
    for k in range(EXPERTS_PER_GROUP):
        x = sc[k]
        for grp in range(1, N_GROUPS):
            x = jnp.where(gsel == grp, sc[grp * 4 + k], x)
        ing.append(x)
    i1 = jnp.zeros_like(gsel)
    w1 = ing[0]
    for k in range(1, 4):
        better = ing[k] > w1
        i1 = jnp.where(better, k, i1)
        w1 = jnp.where(better, ing[k], w1)
    i2 = jnp.full_like(gsel, -1)
    w2 = jnp.full_like(w1, -1.0)
    for k in range(4):
        better = (i1 != k) & (ing[k] > w2)
        i2 = jnp.where(better, k, i2)
        w2 = jnp.where(better, ing[k], w2)
    tot = w1 + w2
    g1, g2 = w1 / tot, w2 / tot
    lo, hi = jnp.minimum(i1, i2), jnp.maximum(i1, i2)
    pair = jnp.where(lo == 0, hi - 1, jnp.where(lo == 1, hi + 1, 5))
    cls = gsel * len(PAIRS) + pair
    cls_ref[...] = cls
    tm = cls.shape[1]

    onehot = lax.broadcasted_iota(jnp.int32, (CLASS_ROWS, tm), 0) == cls
    upper = (lax.broadcasted_iota(jnp.int32, (tm, tm), 0) <= lax.broadcasted_iota(jnp.int32, (tm, tm), 1))
    prefix = jnp.dot(jnp.where(onehot, 1.0, 0.0).astype(BF16), jnp.where(upper, 1.0, 0.0).astype(BF16),
                     preferred_element_type=F32)
    seen = count_ref[:, 0:1]
    rank_ref[...] = jnp.sum(jnp.where(onehot, prefix + seen - 1.0, 0.0), axis=0, keepdims=True).astype(jnp.int32)
    count_ref[...] = count_ref[...] + prefix[:, tm - 1:tm]

    first_is_lo = i1 < i2
    gates = (jnp.where(first_is_lo, g1, g2), jnp.where(first_is_lo, g2, g1))
    rid = lax.broadcasted_iota(jnp.int32, (16, tm), 0)
    pieces = jnp.zeros((16, tm), F32)
    rest = list(gates)
    for part in range(3):
        for k in range(2):
            piece = rest[k].astype(BF16).astype(F32)
            rest[k] = rest[k] - piece
            pieces = jnp.where(rid == 2 * part + k, piece, pieces)
    lane = lax.broadcasted_iota(jnp.int32, (16, GATE_LANES), 1)
    row = lax.broadcasted_iota(jnp.int32, (16, GATE_LANES), 0)
    place = jnp.where((row < 6) & (lane == (row & 1)), 1.0, 0.0).astype(BF16)
    gate_ref[...] = _tn(pieces.astype(BF16), place)


def _router_call(h, w_t, b, tm):
    n = h.shape[0]
    return pl.pallas_call(
        _router_kernel, grid=(n // tm,),
        in_specs=[pl.BlockSpec((tm, D_MODEL), lambda i: (i, 0)),
                  pl.BlockSpec((N_EXPERTS, D_MODEL), lambda i: (0, 0)),
                  pl.BlockSpec((N_EXPERTS, 1), lambda i: (0, 0))],
        out_specs=[pl.BlockSpec((1, tm), lambda i: (0, i)), pl.BlockSpec((1, tm), lambda i: (0, i)),
                   pl.BlockSpec((tm, GATE_LANES), lambda i: (i, 0)),
                   pl.BlockSpec((CLASS_ROWS, 128), lambda i: (0, 0))],
        out_shape=[jax.ShapeDtypeStruct((1, n), jnp.int32), jax.ShapeDtypeStruct((1, n), jnp.int32),
                   jax.ShapeDtypeStruct((n, GATE_LANES), F32), jax.ShapeDtypeStruct((CLASS_ROWS, 128), F32)],
        compiler_params=_params(("arbitrary",)), name="router",
    )(h, w_t, b)


ROW_WIDTH = D_MODEL + GATE_LANES
DMA_UNROLL = 8


def _scatter_kernel(dest_ref, h_ref, g_ref, init_ref, o_ref, row_ref, sem):
    del init_ref
    tm = h_ref.shape[0]
    base = pl.program_id(0) * tm
    row_ref[:, 0:D_MODEL] = h_ref[...]
    row_ref[:, D_MODEL:ROW_WIDTH] = g_ref[...]

    def start(r, carry):
        pltpu.make_async_copy(row_ref.at[pl.ds(r, 1), :], o_ref.at[pl.ds(dest_ref[base + r], 1), :], sem).start()
        return carry

    lax.fori_loop(0, tm, start, 0, unroll=DMA_UNROLL)
    pltpu.make_async_copy(row_ref, o_ref.at[pl.ds(0, tm), :], sem).wait()


def _scatter_call(h, gates, dest, n_out, tm):
    n = h.shape[0]
    return pl.pallas_call(
        _scatter_kernel,
        grid_spec=pltpu.PrefetchScalarGridSpec(
            num_scalar_prefetch=1, grid=(n // tm,),
            in_specs=[pl.BlockSpec((tm, D_MODEL), lambda i, *_: (i, 0)),
                      pl.BlockSpec((tm, GATE_LANES), lambda i, *_: (i, 0)),
                      pl.BlockSpec(memory_space=pl.ANY)],
            out_specs=pl.BlockSpec(memory_space=pl.ANY),
            scratch_shapes=[pltpu.VMEM((tm, ROW_WIDTH), F32), pltpu.SemaphoreType.DMA(())],
        ),
        out_shape=jax.ShapeDtypeStruct((n_out, ROW_WIDTH), F32),
        input_output_aliases={3: 0},
        compiler_params=_params(("arbitrary",)), name="scatter",
    )(dest, h, gates, jnp.zeros((n_out, ROW_WIDTH), F32))


def _gather_ln_kernel(dest_ref, h_ref, y_ref, g_ref, b_ref, *rest, split):
    *o_refs, row_ref, sem = rest
    tm = h_ref.shape[0]
    i = pl.program_id(0)
    base = i * tm

    def start(r, carry):
        pltpu.make_async_copy(y_ref.at[pl.ds(dest_ref[base + r], 1), :], row_ref.at[pl.ds(r, 1), :], sem).start()
        return carry

    lax.fori_loop(0, tm, start, 0, unroll=DMA_UNROLL)
    pltpu.make_async_copy(y_ref.at[pl.ds(0, tm), :], row_ref, sem).wait()
    out = _layer_norm_rows(ALPHA * h_ref[...] + row_ref[...], g_ref[...], b_ref[...])
    if split is None:
        o_refs[0][...] = out
    else:
        @pl.when(i < split)
        def _():
            o_refs[0][...] = out

        @pl.when(i >= split)
        def _():
            o_refs[1][...] = out


def _gather_ln_call(h, ys, dest, g, b, tm, split_rows=None):
    n = h.shape[0]
    row = pl.BlockSpec((tm, D_MODEL), lambda i, *_: (i, 0))
    vec = pl.BlockSpec((1, D_MODEL), lambda i, *_: (0, 0))
    if split_rows is None:
        split, out_specs, out_shape = None, row, jax.ShapeDtypeStruct((n, D_MODEL), F32)
    else:
        split = split_rows // tm
        out_specs = [pl.BlockSpec((tm, D_MODEL), lambda i, *_: (jnp.minimum(i, split - 1), 0)),
                     pl.BlockSpec((tm, D_MODEL), lambda i, *_: (jnp.maximum(i - split, 0), 0))]
        out_shape = [jax.ShapeDtypeStruct((split_rows, D_MODEL), F32),
                     jax.ShapeDtypeStruct((n - split_rows, D_MODEL), F32)]
    return pl.pallas_call(
        functools.partial(_gather_ln_kernel, split=split),
        grid_spec=pltpu.PrefetchScalarGridSpec(
            num_scalar_prefetch=1, grid=(n // tm,),
            in_specs=[row, pl.BlockSpec(memory_space=pl.ANY), vec, vec],
            out_specs=out_specs,
            scratch_shapes=[pltpu.VMEM((tm, D_MODEL), F32), pltpu.SemaphoreType.DMA(())],
        ),
        out_shape=out_shape,
        compiler_params=_params(("arbitrary",)), name="gather_ln",
    )(dest, h, ys, g.reshape(1, -1), b.reshape(1, -1))


def _gelu_tanh(x):
    return 0.5 * x * (1.0 + jnp.tanh(0.7978845608028654 * (x + 0.044715 * x * x * x)))


def _moe_kernel(ea_ref, eb_ref, used_ref, x_ref, w1a_ref, w2a_ref, w1b_ref, w2b_ref, o_ref):
    @pl.when(pl.program_id(0) < used_ref[0])
    def _():
        x = x_ref[:, 0:D_MODEL].astype(BF16)

        def expert(w1_ref, w2_ref):
            hid = _gelu_tanh(jnp.dot(x, w1_ref[...], preferred_element_type=F32))
            return jnp.dot(hid.astype(BF16), w2_ref[...], preferred_element_type=F32)

        g = x_ref[:, D_MODEL:ROW_WIDTH]
        o_ref[...] = g[:, 0:1] * expert(w1a_ref, w2a_ref) + g[:, 1:2] * expert(w1b_ref, w2b_ref)

    @pl.when(pl.program_id(0) >= used_ref[0])
    def _():
        o_ref[...] = jnp.zeros_like(o_ref)


def _moe_call(xs, blk_ea, blk_eb, n_used, w1, w2, tm):
    npad = xs.shape[0]
    w1a = pl.BlockSpec((None, D_MODEL, D_EXPERT), lambda i, ea, eb, u: (ea[i], 0, 0))
    w2a = pl.BlockSpec((None, D_EXPERT, D_MODEL), lambda i, ea, eb, u: (ea[i], 0, 0))
    w1b = pl.BlockSpec((None, D_MODEL, D_EXPERT), lambda i, ea, eb, u: (eb[i], 0, 0))
    w2b = pl.BlockSpec((None, D_EXPERT, D_MODEL), lambda i, ea, eb, u: (eb[i], 0, 0))
    return pl.pallas_call(
        _moe_kernel,
        grid_spec=pltpu.PrefetchScalarGridSpec(
            num_scalar_prefetch=3, grid=(npad // tm,),
            in_specs=[pl.BlockSpec((tm, ROW_WIDTH), lambda i, *_: (i, 0)), w1a, w2a, w1b, w2b],
            out_specs=pl.BlockSpec((tm, D_MODEL), lambda i, *_: (i, 0)),
        ),
        out_shape=jax.ShapeDtypeStruct((npad, D_MODEL), F32),
        compiler_params=_params(("arbitrary",), VMEM_LIMIT), name="moe",
    )(blk_ea, blk_eb, n_used, xs, w1, w2, w1, w2)


def _routing_tables(cls, rank, counts, tm):
    nblk = cls.shape[0] // tm + N_CLASSES
    padded = (counts + tm - 1) // tm * tm
    pad_end = jnp.cumsum(padded)
    dest = (pad_end - padded)[cls] + rank
    n_used = pad_end[-1] // tm
    blk = jnp.arange(nblk, dtype=jnp.int32)
    blk_cls = jnp.searchsorted(pad_end, jnp.minimum(blk, n_used - 1) * tm, side="right").astype(jnp.int32)
    blk_cls = jnp.minimum(blk_cls, N_CLASSES - 1)
    pairs = jnp.asarray(PAIRS, jnp.int32)
    grp, pr = blk_cls // len(PAIRS), blk_cls % len(PAIRS)
    blk_ea = grp * EXPERTS_PER_GROUP + pairs[pr, 0]
    blk_eb = grp * EXPERTS_PER_GROUP + pairs[pr, 1]
    return dest, nblk * tm, blk_ea, blk_eb, n_used.reshape(1).astype(jnp.int32)


def _block_flags(seq_lens, t):
    first, last = [], []
    for length in seq_lens:
        nb = length // t
        first += [1] + [0] * (nb - 1)
        last += [0] * (nb - 1) + [1]
    return jnp.asarray(np.array(first, np.int32)), jnp.asarray(np.array(last, np.int32))


def _trunk(xa, xb, seq_lens, p, *, tm, tm_proj, t_conv, t_gla, tm_moe):
    (ln_in_g, ln_in_b, w_in, dw_w, dw_b, conv_ln_g, conv_ln_b, w_conv_out, hg_lower, hg_norm_g, w_hg_out,
     w_o, ln1_g, ln1_b, w_router, b_router, w_e1, w_e2, ln2_g, ln2_b) = p
    depth = w_in.shape[0]
    conv_first, conv_last = _block_flags(seq_lens, t_conv)
    gla_first, gla_last = _block_flags(seq_lens, t_gla)

    lb_p = jax.nn.softmax(hg_lower.astype(F32), axis=0)
    lb_all = jnp.clip(jnp.cumsum(lb_p, axis=0) - lb_p[0:1], 0.0, 1.0)
    w_in_g = w_in.reshape(depth, D_MODEL, 9, D_MODEL)[:, :, jnp.asarray([0, 1, 2, 5, 6, 3, 4, 7, 8])]
    w_in_b = w_in_g.reshape(depth, D_MODEL, 9 * D_MODEL).astype(BF16)
    w_router_t = w_router.T.astype(BF16)
    b_router_c = b_router.reshape(N_EXPERTS, 1).astype(F32)

    h = _ln_call(xa, xb, ln_in_g, ln_in_b, tm)
    for l in range(depth):
        u, qvo, f, gab = _proj_call(h, w_in_b[l], lb_all[l].reshape(2, 1, D_MODEL), tm_proj)
        ua = _conv_call(u, conv_first, conv_last, dw_w[l], dw_b[l], conv_ln_g[l], conv_ln_b[l], t_conv)
        o_bw = _gla_call(gla_last, qvo, f, 1, t_gla)
        ob = _gla_call(gla_first, qvo, f, 0, t_gla, obw=o_bw, g=hg_norm_g[l].astype(F32))
        h1 = _merge_call(h, ua, ob, gab, w_conv_out[l].astype(BF16), w_hg_out[l].astype(BF16),
                         w_o[l].astype(BF16), ln1_g[l], ln1_b[l], tm)
        cls, rank, gates, counts = _router_call(h1, w_router_t, b_router_c, tm)
        dest, n_rows, blk_ea, blk_eb, n_used = _routing_tables(
            cls[0], rank[0], counts[:N_CLASSES, 0].astype(jnp.int32), tm_moe)
        xs = _scatter_call(h1, gates, dest, n_rows, tm)
        ys = _moe_call(xs, blk_ea, blk_eb, n_used, w_e1[l].astype(BF16), w_e2[l].astype(BF16), tm_moe)
        h = _gather_ln_call(h1, ys, dest, ln2_g[l], ln2_b[l], tm,
                            split_rows=xa.shape[0] if l == depth - 1 else None)
    return h


def kernel(x_prompt, x_sample, ln_in_g, ln_in_b, w_in, dw_w, dw_b, conv_ln_g, conv_ln_b, w_conv_out, hg_lower,
           hg_norm_g, w_hg_out, w_o, ln1_g, ln1_b, w_router, b_router, w_e1, w_e2, ln2_g, ln2_b):
    bp, lp, d = x_prompt.shape
    bs, ls, _ = x_sample.shape
    seq_lens = (lp,) * bp + (ls,) * bs
    p = (ln_in_g, ln_in_b, w_in, dw_w, dw_b, conv_ln_g, conv_ln_b, w_conv_out, hg_lower, hg_norm_g, w_hg_out,
         w_o, ln1_g, ln1_b, w_router, b_router, w_e1, w_e2, ln2_g, ln2_b)
    yp, ys = _trunk(x_prompt.reshape(bp * lp, d), x_sample.reshape(bs * ls, d), seq_lens, p,
                    tm=512, tm_proj=1024, t_conv=512, t_gla=2048, tm_moe=256)
    return (yp.reshape(bp, lp, d), ys.reshape(bs, ls, d))
```

```python
import functools
import math

import numpy as np
import jax
import jax.numpy as jnp
from jax import lax
from jax.experimental import pallas as pl
from jax.experimental.pallas import tpu as pltpu

F32 = jnp.float32
BF16 = jnp.bfloat16

D_MODEL = 1024
DEPTH = 4
HEAD_DIM = 128
N_HEADS = D_MODEL // HEAD_DIM
CONV_WIDTH = 31
CONV_PAD = CONV_WIDTH // 2
HALO = 16
CHUNK = 64
GLA_GROUP = 4
EXP_CLAMP = 60.0
N_EXPERTS = 16
N_GROUPS = 4
EXPERTS_PER_GROUP = 4
D_EXPERT = 2 * D_MODEL
PAIRS = ((0, 1), (0, 2), (0, 3), (1, 2), (1, 3), (2, 3))
N_CLASSES = N_GROUPS * len(PAIRS)
ALPHA = (2 * DEPTH) ** 0.25
LN_EPS = 1e-5
RMS_EPS = 1e-6
VMEM_LIMIT = 56 * 1024 * 1024


def _params(sem, vmem=None):
    return pltpu.CompilerParams(dimension_semantics=sem, vmem_limit_bytes=vmem)


def _sigmoid(x):
    return 1.0 / (1.0 + jnp.exp(-x))


def _layer_norm_rows(x, g, b):
    mu = jnp.mean(x, axis=-1, keepdims=True)
    xc = x - mu
    var = jnp.mean(xc * xc, axis=-1, keepdims=True)
    return xc * lax.rsqrt(var + LN_EPS) * g + b


def _ln_kernel(xa_ref, xb_ref, g_ref, b_ref, o_ref, *, split):
    i = pl.program_id(0)

    @pl.when(i < split)
    def _():
        o_ref[...] = _layer_norm_rows(xa_ref[...], g_ref[...], b_ref[...])

    @pl.when(i >= split)
    def _():
        o_ref[...] = _layer_norm_rows(xb_ref[...], g_ref[...], b_ref[...])


def _ln_call(xa, xb, g, b, tm):
    na, nb = xa.shape[0], xb.shape[0]
    split = na // tm
    vec = pl.BlockSpec((1, D_MODEL), lambda i: (0, 0))
    return pl.pallas_call(
        functools.partial(_ln_kernel, split=split), grid=((na + nb) // tm,),
        in_specs=[pl.BlockSpec((tm, D_MODEL), lambda i: (jnp.minimum(i, split - 1), 0)),
                  pl.BlockSpec((tm, D_MODEL), lambda i: (jnp.maximum(i - split, 0), 0)), vec, vec],
        out_specs=pl.BlockSpec((tm, D_MODEL), lambda i: (i, 0)),
        out_shape=jax.ShapeDtypeStruct((na + nb, D_MODEL), F32),
        compiler_params=_params(("arbitrary",)), name="ln_in",
    )(xa, xb, g.reshape(1, -1), b.reshape(1, -1))


PROJ_COLS = 256


def _proj_kernel(x_ref, wv_ref, w_ref, lb_ref, u_ref, qvo_ref, f_ref, gab_ref):
    j = pl.program_id(1)
    x = x_ref[...].astype(BF16)
    pieces = [slice(c * PROJ_COLS, (c + 1) * PROJ_COLS) for c in range(D_MODEL // PROJ_COLS)]

    def mm(ref, cols):
        return jnp.dot(x, ref[:, cols], preferred_element_type=F32)

    def heads_store(ref, cols, val):
        for k in range(PROJ_COLS // HEAD_DIM):
            ref[cols.start // HEAD_DIM + k] = val[:, k * HEAD_DIM:(k + 1) * HEAD_DIM].astype(ref.dtype)

    @pl.when(j == 0)
    def _():
        for cols in pieces:
            u_ref[:, cols] = (mm(wv_ref, cols) * _sigmoid(mm(w_ref, cols))).astype(BF16)

    @pl.when(j == 1)
    def _():
        for cols in pieces:
            a = mm(w_ref, cols)
            heads_store(qvo_ref, cols, a * _sigmoid(a) * HEAD_DIM ** -0.5)

    @pl.when(j == 2)
    def _():
        for cols in pieces:
            heads_store(qvo_ref, cols, mm(w_ref, cols))

    @pl.when(j == 3)
    def _():
        for cols in pieces:
            a = mm(w_ref, cols)
            heads_store(qvo_ref, cols, a * _sigmoid(a))

    @pl.when((j == 4) | (j == 5))
    def _():
        for cols in pieces:
            t = jnp.exp(-mm(w_ref, cols))
            f = (1.0 + lb_ref[:, cols] * jnp.minimum(t, math.exp(EXP_CLAMP))) / (1.0 + t)
            heads_store(f_ref, cols, jnp.minimum(f, 1.0))

    @pl.when(j >= 6)
    def _():
        for cols in pieces:
            gab_ref[:, cols] = _sigmoid(mm(w_ref, cols)).astype(BF16)


def _proj_group(j):
    t = j + 1
    return jnp.where((t == 3) | (t == 4), t + 2, jnp.where((t == 5) | (t == 6), t - 2, t))


def _proj_call(h, w, lb, tm):
    n = h.shape[0]
    return pl.pallas_call(
        _proj_kernel, grid=(n // tm, 8),
        in_specs=[
            pl.BlockSpec((tm, D_MODEL), lambda i, j: (i, 0)),
            pl.BlockSpec((None, D_MODEL, D_MODEL), lambda i, j: (0, 0, 0)),
            pl.BlockSpec((None, D_MODEL, D_MODEL), lambda i, j: (_proj_group(j), 0, 0)),
            pl.BlockSpec((None, 1, D_MODEL), lambda i, j: (jnp.clip(j - 4, 0, 1), 0, 0)),
        ],
        out_specs=[
            pl.BlockSpec((tm, D_MODEL), lambda i, j: (i, 0)),
            pl.BlockSpec((None, N_HEADS, tm, HEAD_DIM), lambda i, j: (jnp.clip(j - 1, 0, 2), 0, i, 0)),
            pl.BlockSpec((None, N_HEADS, tm, HEAD_DIM), lambda i, j: (jnp.clip(j - 4, 0, 1), 0, i, 0)),
            pl.BlockSpec((None, tm, D_MODEL), lambda i, j: (jnp.clip(j - 6, 0, 1), i, 0)),
        ],
        out_shape=[
            jax.ShapeDtypeStruct((n, D_MODEL), BF16),
            jax.ShapeDtypeStruct((3, N_HEADS, n, HEAD_DIM), BF16),
            jax.ShapeDtypeStruct((2, N_HEADS, n, HEAD_DIM), F32),
            jax.ShapeDtypeStruct((2, n, D_MODEL), BF16),
        ],
        compiler_params=_params(("parallel", "arbitrary"), VMEM_LIMIT), name="proj",
    )(h, w, w, lb)


def _conv_kernel(first_ref, last_ref, up_ref, u_ref, un_ref, w_ref, b_ref, g_ref, beta_ref, o_ref,
                 buf_ref, acc_ref, *, rows):
    i = pl.program_id(0)
    t = u_ref.shape[0]
    nslab = D_MODEL // 128
    prev = jnp.where(first_ref[i] == 1, 0.0, up_ref[...].astype(F32))
    nxt = jnp.where(last_ref[i] == 1, 0.0, un_ref[...].astype(F32))
    for c in range(nslab):
        cols = slice(c * 128, (c + 1) * 128)
        buf_ref[c, 0:HALO, :] = prev[:, cols]
        buf_ref[c, HALO:HALO + t, :] = u_ref[:, cols].astype(F32)
        buf_ref[c, HALO + t:2 * HALO + t, :] = nxt[:, cols]

    def slab(c, carry):
        for rb in range(t // rows):
            acc = jnp.zeros((rows, 128), F32)
            for k in range(CONV_WIDTH):
                r0 = rb * rows + HALO - CONV_PAD + k
                acc = acc + w_ref[c, k:k + 1, :] * buf_ref[c, r0:r0 + rows, :]
            acc_ref[c, rb * rows:(rb + 1) * rows, :] = acc
        return carry

    lax.fori_loop(0, nslab, slab, 0)
    x = jnp.concatenate([acc_ref[c] for c in range(nslab)], axis=-1) + b_ref[...]
    y = _layer_norm_rows(x, g_ref[...], beta_ref[...])
    o_ref[...] = (y * _sigmoid(y)).astype(BF16)


def _conv_call(u, first, last, w, b, g, beta, t):
    n = u.shape[0]
    nb, hb = n // t, t // HALO
    nslab = D_MODEL // 128
    wpad = jnp.zeros((32, D_MODEL), F32).at[:CONV_WIDTH].set(w).reshape(32, nslab, 128).transpose(1, 0, 2)
    vec = pl.BlockSpec((1, D_MODEL), lambda i, *_: (0, 0))
    return pl.pallas_call(
        functools.partial(_conv_kernel, rows=32),
        grid_spec=pltpu.PrefetchScalarGridSpec(
            num_scalar_prefetch=2, grid=(nb,),
            in_specs=[
                pl.BlockSpec((HALO, D_MODEL), lambda i, *_: (jnp.maximum(i * hb - 1, 0), 0)),
                pl.BlockSpec((t, D_MODEL), lambda i, *_: (i, 0)),
                pl.BlockSpec((HALO, D_MODEL), lambda i, *_: (jnp.minimum((i + 1) * hb, nb * hb - 1), 0)),
                pl.BlockSpec((nslab, 32, 128), lambda i, *_: (0, 0, 0)),
                vec, vec, vec,
            ],
            out_specs=pl.BlockSpec((t, D_MODEL), lambda i, *_: (i, 0)),
            scratch_shapes=[pltpu.VMEM((nslab, t + 2 * HALO, 128), F32), pltpu.VMEM((nslab, t, 128), F32)],
        ),
        out_shape=jax.ShapeDtypeStruct((n, D_MODEL), BF16),
        compiler_params=_params(("parallel",), VMEM_LIMIT), name="conv",
    )(first, last, u, u, u, wpad, b.reshape(1, -1), g.reshape(1, -1), beta.reshape(1, -1))


def _mul(a, b):
    if a is None:
        return b
    if b is None:
        return a
    return a * b


def _cat(tiles):
    return jnp.concatenate(tiles, axis=0)


def _nt(a, b):
    return lax.dot_general(a, b, (((1,), (1,)), ((), ())), preferred_element_type=F32)


def _tn(a, b):
    return lax.dot_general(a, b, (((0,), (0,)), ((), ())), preferred_element_type=F32)


def _gla_masks():
    i = lax.broadcasted_iota(jnp.int32, (CHUNK, CHUNK), 0)
    j = lax.broadcasted_iota(jnp.int32, (CHUNK, CHUNK), 1)
    rt, rs = i & 7, j & 7
    return [(rt >> 1) == (rs >> 1), (rt >> 2) == (rs >> 2), None]


def _block_products(tot, rev):
    ri = lax.broadcasted_iota(jnp.int32, tot.shape, 0)
    r = (7 - ri) if rev else ri

    def sh(x, d):
        return pltpu.roll(x, ((-d) if rev else d) % 8, 0)

    one = jnp.ones_like(tot)
    zero = jnp.zeros_like(tot)
    b0, b1, b2 = (r & 1) == 1, (r & 2) == 2, (r & 4) == 4
    m4 = r & 3
    d1, u1 = sh(tot, 1), sh(tot, -1)

    def excl_prefix(width):
        m = r & (width - 1)
        e = jnp.where(m >= 1, d1, one)
        step = 1
        while step < width:
            e = e * jnp.where(m >= step, sh(e, step), one)
            step *= 2
        return e

    def excl_suffix(width):
        m = r & (width - 1)
        e = jnp.where(m <= width - 2, u1, one)
        step = 1
        while step < width:
            e = e * jnp.where(m <= width - 1 - step, sh(e, -step), one)
            step *= 2
        return e

    lvl3 = (jnp.where(b0, one, zero), jnp.where(b0, zero, one))
    lvl4 = (jnp.where(b1, jnp.where(b0, d1, one), zero), jnp.where(b1, zero, jnp.where(b0, one, u1)))
    lvl5 = (jnp.where(b2, excl_prefix(4), zero), jnp.where(b2, zero, excl_suffix(4)))
    state = (excl_prefix(8), excl_suffix(8))
    del m4
    return [lvl3, lvl4, lvl5, state]


def _gla_prep(fs, qs, vs, rev):
    n = 8
    ks = [1.0 - f for f in fs]
    fi = [list(fs)]
    fi.append([fs[a] * fs[a - 1] if a & 1 else fs[a] for a in range(n)])
    fi.append([fi[1][a] * fi[1][(a & ~3) + 1] if a & 2 else fi[1][a] for a in range(n)])
    fi.append([fi[2][a] * fi[2][3] if a & 4 else fi[2][a] for a in range(n)])
    ge = [[None] * n]
    ge.append([fs[a + 1] if not a & 1 else None for a in range(n)])
    ge.append([_mul(ge[1][a], fi[1][(a & ~3) + 3]) if not a & 2 else ge[1][a] for a in range(n)])
    ge.append([_mul(ge[2][a], fi[2][7]) if not a & 4 else ge[2][a] for a in range(n)])

    low = [jnp.sum(qs[a] * ks[a], axis=-1, keepdims=True) * vs[a] for a in range(n)]
    for lvl in range(3):
        half = 1 << lvl
        kt = [None if (a >> lvl) & 1 else _mul(ks[a], ge[lvl][a]) for a in range(n)]
        for a in range(n):
            if not (a >> lvl) & 1:
                continue
            qt = qs[a] * fi[lvl][a]
            first = a & ~(2 * half - 1)
            for s in range(first, first + half):
                low[a] = low[a] + jnp.sum(qt * kt[s], axis=-1, keepdims=True) * vs[s]
    low = _cat(low)

    levels = []
    qf = [qs[a] * fi[3][a] for a in range(n)]
    kg = [_mul(ks[a], ge[3][a]) for a in range(n)]
    tot = fi[3][7]
    facs = _block_products(tot, rev)
    for lvl in range(3):
        rq, rk = facs[lvl]
        levels.append((_cat([x * rq for x in qf]).astype(BF16), _cat([x * rk for x in kg]).astype(BF16)))
    rq, rk = facs[3]
    q_in = _cat([x * rq for x in qf]).astype(BF16)
    k_out = _cat([x * rk for x in kg]).astype(BF16)
    last = 0 if rev else 7
    decay = (rq * tot)[last:last + 1, :]
    return levels, (_cat(vs).astype(BF16), q_in, k_out, decay, low)


def _gla_kernel(flag_ref, q_ref, v_ref, f_ref, *rest, rev, final):
    if final:
        obw_ref, og_ref, g_ref, o_ref, st_ref, qs_ref, vs_ref, os_ref = rest
    else:
        o_ref, st_ref, qs_ref, vs_ref = rest
        os_ref = o_ref
    nb = pl.num_programs(1)
    tb = pl.program_id(1)
    blk = (nb - 1 - tb) if rev else tb
    t = q_ref.shape[0]
    nchunks = t // CHUNK

    @pl.when(flag_ref[blk] == 1)
    def _():
        st_ref[...] = jnp.zeros_like(st_ref)

    qs_ref[...] = q_ref[...].astype(F32)
    vs_ref[...] = v_ref[...].astype(F32)
    masks = _gla_masks()
    order = list(range(7, -1, -1)) if rev else list(range(8))

    def rows_of(ci):
        c0 = ((nchunks - 1 - ci) if rev else ci) * CHUNK
        if not isinstance(c0, int):
            c0 = pl.multiple_of(c0, CHUNK)
        return [pl.ds(c0 + a, 8, stride=8) for a in order]

    def prep(ci):
        rows = rows_of(ci)
        return _gla_prep([f_ref[rw, :] for rw in rows], [qs_ref[rw, :] for rw in rows],
                         [vs_ref[rw, :] for rw in rows], rev)

    grp = min(GLA_GROUP, nchunks)
    ngroups = nchunks // grp

    def operands(g):
        return [prep(g * grp + u) for u in range(grp)]

    def products(ops):
        prods = []
        for levels, (vb, q_in, k_out, decay, low) in ops:
            att = None
            for (qt, kt), mask in zip(levels, masks):
                part = _nt(qt, kt)
                if mask is not None:
                    part = jnp.where(mask, part, 0.0)
                att = part if att is None else att + part
            prods.append((att.astype(BF16), vb, q_in, _tn(vb, k_out), decay, low))
        return prods

    def finish(g, prods, st):
        for u, (att, vb, q_in, kv, decay, low) in enumerate(prods):
            out = jnp.dot(att, vb, preferred_element_type=F32) + _nt(q_in, st.astype(BF16)) + low
            st = st * decay + kv
            for a, rw in enumerate(rows_of(g * grp + u)):
                os_ref[rw, :] = out[8 * a:8 * a + 8]
        return st

    st = st_ref[...]
    ops = [operands(g) for g in range(min(2, ngroups))]
    prods = products(ops.pop(0))
    for g in range(ngroups):
        st = finish(g, prods, st)
        if ops:
            prods = products(ops.pop(0))
        if g + 2 < ngroups:
            ops.append(operands(g + 2))
    st_ref[...] = st

    if final:
        o = os_ref[...] + obw_ref[...]
        o = o * lax.rsqrt(jnp.mean(o * o, axis=-1, keepdims=True) + RMS_EPS) * g_ref[...]
        o_ref[...] = (o * og_ref[...].astype(F32)).astype(BF16)


def _gla_call(flags, qvo, f, direction, t, obw=None, g=None):
    n = qvo.shape[2]
    nb = n // t
    rev = direction == 1
    final = not rev

    def tmap(hd, tb, *_):
        return (nb - 1 - tb) if rev else tb

    def sel(k):
        return pl.BlockSpec((None, None, t, HEAD_DIM), lambda hd, tb, *_: (k, hd, tmap(hd, tb), 0))

    in_specs = [sel(0), sel(1), sel(direction)]
    args = [qvo, qvo, f]
    scratch = [pltpu.VMEM((HEAD_DIM, HEAD_DIM), F32), pltpu.VMEM((t, HEAD_DIM), F32),
               pltpu.VMEM((t, HEAD_DIM), F32)]
    if final:
        in_specs += [pl.BlockSpec((None, t, HEAD_DIM), lambda hd, tb, *_: (hd, tb, 0)), sel(2),
                     pl.BlockSpec((None, 1, HEAD_DIM), lambda hd, tb, *_: (hd, 0, 0))]
        args += [obw, qvo, g.reshape(N_HEADS, 1, HEAD_DIM)]
        scratch.append(pltpu.VMEM((t, HEAD_DIM), F32))
    return pl.pallas_call(
        functools.partial(_gla_kernel, rev=rev, final=final),
        grid_spec=pltpu.PrefetchScalarGridSpec(
            num_scalar_prefetch=1, grid=(N_HEADS, nb), in_specs=in_specs,
            out_specs=pl.BlockSpec((None, t, HEAD_DIM), lambda hd, tb, *_: (hd, tmap(hd, tb), 0)),
            scratch_shapes=scratch,
        ),
        out_shape=jax.ShapeDtypeStruct((N_HEADS, n, HEAD_DIM), BF16 if final else F32),
        compiler_params=_params(("parallel", "arbitrary"), VMEM_LIMIT),
        name="gla_fwd" if final else "gla_bwd",
    )(flags, *args)


def _merge_kernel(h_ref, ua_ref, ob_ref, gab_ref, wc_ref, whg_ref, wo_ref, g_ref, b_ref, o_ref):
    ob = jnp.concatenate([ob_ref[hd] for hd in range(N_HEADS)], axis=-1)
    branch_a = jnp.dot(ua_ref[...], wc_ref[...], preferred_element_type=F32)
    branch_b = jnp.dot(ob, whg_ref[...], preferred_element_type=F32)
    merged = gab_ref[0].astype(F32) * branch_a + gab_ref[1].astype(F32) * branch_b
    mix = jnp.dot(merged.astype(BF16), wo_ref[...], preferred_element_type=F32)
    o_ref[...] = _layer_norm_rows(ALPHA * h_ref[...] + mix, g_ref[...], b_ref[...])


def _merge_call(h, ua, ob, gab, wc, whg, wo, g, b, tm):
    n = h.shape[0]
    row = pl.BlockSpec((tm, D_MODEL), lambda i: (i, 0))
    mat = pl.BlockSpec((D_MODEL, D_MODEL), lambda i: (0, 0))
    vec = pl.BlockSpec((1, D_MODEL), lambda i: (0, 0))
    return pl.pallas_call(
        _merge_kernel, grid=(n // tm,),
        in_specs=[row, row, pl.BlockSpec((N_HEADS, tm, HEAD_DIM), lambda i: (0, i, 0)),
                  pl.BlockSpec((2, tm, D_MODEL), lambda i: (0, i, 0)), mat, mat, mat, vec, vec],
        out_specs=row, out_shape=jax.ShapeDtypeStruct((n, D_MODEL), F32),
        compiler_params=_params(("parallel",), VMEM_LIMIT), name="merge",
    )(h, ua, ob, gab, wc, whg, wo, g.reshape(1, -1), b.reshape(1, -1))


GATE_LANES = 128
CLASS_ROWS = 32


def _router_kernel(h_ref, w_ref, b_ref, cls_ref, rank_ref, gate_ref, count_ref):
    @pl.when(pl.program_id(0) == 0)
    def _():
        count_ref[...] = jnp.zeros_like(count_ref)

    logits = lax.dot_general(w_ref[...], h_ref[...].astype(BF16), (((1,), (1,)), ((), ())),
                             preferred_element_type=F32) + b_ref[...]
    rows = [logits[e:e + 1, :] for e in range(N_EXPERTS)]
    mx = functools.reduce(jnp.maximum, rows)
    ex = [jnp.exp(x - mx) for x in rows]
    den = functools.reduce(lambda a, b: a + b, ex)
    sc = [e / den for e in ex]
    gscore = []
    for grp in range(N_GROUPS):
        s = sc[grp * 4:grp * 4 + 4]
        gscore.append(functools.reduce(jnp.maximum, [s[a] + s[b] for a, b in PAIRS]))
    gsel = jnp.zeros_like(gscore[0], dtype=jnp.int32)
    best = gscore[0]
    for grp in range(1, N_GROUPS):
        better = gscore[grp] > best
        gsel = jnp.where(better, grp, gsel)
        best = jnp.where(better, gscore[grp], best)
    ing = []
    for k in range(EXPERTS_PER_GROUP):
        x = sc[k]
        for grp in range(1, N_GROUPS):
            x = jnp.where(gsel == grp, sc[grp * 4 + k], x)
        ing.append(x)
    i1 = jnp.zeros_like(gsel)
    w1 = ing[0]
    for k in range(1, 4):
        better = ing[k] > w1
        i1 = jnp.where(better, k, i1)
        w1 = jnp.where(better, ing[k], w1)
    i2 = jnp.full_like(gsel, -1)
    w2 = jnp.full_like(w1, -1.0)
    for k in range(4):
        better = (i1 != k) & (ing[k] > w2)
        i2 = jnp.where(better, k, i2)
        w2 = jnp.where(better, ing[k], w2)
    tot = w1 + w2
    g1, g2 = w1 / tot, w2 / tot
    lo, hi = jnp.minimum(i1, i2), jnp.maximum(i1, i2)
    pair = jnp.where(lo == 0, hi - 1, jnp.where(lo == 1, hi + 1, 5))
    cls = gsel * len(PAIRS) + pair
    cls_ref[...] = cls
    tm = cls.shape[1]

    onehot = lax.broadcasted_iota(jnp.int32, (CLASS_ROWS, tm), 0) == cls
    upper = (lax.broadcasted_iota(jnp.int32, (tm, tm), 0) <= lax.broadcasted_iota(jnp.int32, (tm, tm), 1))
    prefix = jnp.dot(jnp.where(onehot, 1.0, 0.0).astype(BF16), jnp.where(upper, 1.0, 0.0).astype(BF16),
                     preferred_element_type=F32)
    seen = count_ref[:, 0:1]
    rank_ref[...] = jnp.sum(jnp.where(onehot, prefix + seen - 1.0, 0.0), axis=0, keepdims=True).astype(jnp.int32)
    count_ref[...] = count_ref[...] + prefix[:, tm - 1:tm]

    first_is_lo = i1 < i2
    gates = (jnp.where(first_is_lo, g1, g2), jnp.where(first_is_lo, g2, g1))
    rid = lax.broadcasted_iota(jnp.int32, (16, tm), 0)
    pieces = jnp.zeros((16, tm), F32)
    rest = list(gates)
    for part in range(3):
        for k in range(2):
            piece = rest[k].astype(BF16).astype(F32)
            rest[k] = rest[k] - piece
            pieces = jnp.where(rid == 2 * part + k, piece, pieces)
    lane = lax.broadcasted_iota(jnp.int32, (16, GATE_LANES), 1)
    row = lax.broadcasted_iota(jnp.int32, (16, GATE_LANES), 0)
    place = jnp.where((row < 6) & (lane == (row & 1)), 1.0, 0.0).astype(BF16)
    gate_ref[...] = _tn(pieces.astype(BF16), place)


def _router_call(h, w_t, b, tm):
    n = h.shape[0]
    return pl.pallas_call(
        _router_kernel, grid=(n // tm,),
        in_specs=[pl.BlockSpec((tm, D_MODEL), lambda i: (i, 0)),
                  pl.BlockSpec((N_EXPERTS, D_MODEL), lambda i: (0, 0)),
                  pl.BlockSpec((N_EXPERTS, 1), lambda i: (0, 0))],
        out_specs=[pl.BlockSpec((1, tm), lambda i: (0, i)), pl.BlockSpec((1, tm), lambda i: (0, i)),
                   pl.BlockSpec((tm, GATE_LANES), lambda i: (i, 0)),
                   pl.BlockSpec((CLASS_ROWS, 128), lambda i: (0, 0))],
        out_shape=[jax.ShapeDtypeStruct((1, n), jnp.int32), jax.ShapeDtypeStruct((1, n), jnp.int32),
                   jax.ShapeDtypeStruct((n, GATE_LANES), F32), jax.ShapeDtypeStruct((CLASS_ROWS, 128), F32)],
        compiler_params=_params(("arbitrary",)), name="router",
    )(h, w_t, b)


ROW_WIDTH = D_MODEL + GATE_LANES
DMA_UNROLL = 8


def _scatter_kernel(dest_ref, h_ref, g_ref, init_ref, o_ref, row_ref, sem):
    del init_ref
    tm = h_ref.shape[0]
    base = pl.program_id(0) * tm
    row_ref[:, 0:D_MODEL] = h_ref[...]
    row_ref[:, D_MODEL:ROW_WIDTH] = g_ref[...]

    def start(t, carry):
        r0 = pl.multiple_of(t * 8, 8)
        for k in range(8):
            pltpu.make_async_copy(row_ref.at[pl.ds(r0 + k, 1), :],
                                  o_ref.at[pl.ds(dest_ref[base + r0 + k], 1), :], sem).start()
        return carry

    lax.fori_loop(0, tm // 8, start, 0)
    pltpu.make_async_copy(row_ref, o_ref.at[pl.ds(0, tm), :], sem).wait()


def _scatter_call(h, gates, dest, n_out, tm):
    n = h.shape[0]
    return pl.pallas_call(
        _scatter_kernel,
        grid_spec=pltpu.PrefetchScalarGridSpec(
            num_scalar_prefetch=1, grid=(n // tm,),
            in_specs=[pl.BlockSpec((tm, D_MODEL), lambda i, *_: (i, 0)),
                      pl.BlockSpec((tm, GATE_LANES), lambda i, *_: (i, 0)),
                      pl.BlockSpec(memory_space=pl.ANY)],
            out_specs=pl.BlockSpec(memory_space=pl.ANY),
            scratch_shapes=[pltpu.VMEM((tm, ROW_WIDTH), F32), pltpu.SemaphoreType.DMA(())],
        ),
        out_shape=jax.ShapeDtypeStruct((n_out, ROW_WIDTH), F32),
        input_output_aliases={3: 0},
        compiler_params=_params(("arbitrary",)), name="scatter",
    )(dest, h, gates, jnp.zeros((n_out, ROW_WIDTH), F32))


def _gather_ln_kernel(dest_ref, h_ref, y_ref, g_ref, b_ref, *rest, split):
    *o_refs, row_ref, sem = rest
    tm = h_ref.shape[0]
    i = pl.program_id(0)
    base = i * tm

    def start(r, carry):
        pltpu.make_async_copy(y_ref.at[pl.ds(dest_ref[base + r], 1), :], row_ref.at[pl.ds(r, 1), :], sem).start()
        return carry

    lax.fori_loop(0, tm, start, 0, unroll=DMA_UNROLL)
    pltpu.make_async_copy(y_ref.at[pl.ds(0, tm), :], row_ref, sem).wait()
    out = _layer_norm_rows(ALPHA * h_ref[...] + row_ref[...], g_ref[...], b_ref[...])
    if split is None:
        o_refs[0][...] = out
    else:
        @pl.when(i < split)
        def _():
            o_refs[0][...] = out

        @pl.when(i >= split)
        def _():
            o_refs[1][...] = out


def _gather_ln_call(h, ys, dest, g, b, tm, split_rows=None):
    n = h.shape[0]
    row = pl.BlockSpec((tm, D_MODEL), lambda i, *_: (i, 0))
    vec = pl.BlockSpec((1, D_MODEL), lambda i, *_: (0, 0))
    if split_rows is None:
        split, out_specs, out_shape = None, row, jax.ShapeDtypeStruct((n, D_MODEL), F32)
    else:
        split = split_rows // tm
        out_specs = [pl.BlockSpec((tm, D_MODEL), lambda i, *_: (jnp.minimum(i, split - 1), 0)),
                     pl.BlockSpec((tm, D_MODEL), lambda i, *_: (jnp.maximum(i - split, 0), 0))]
        out_shape = [jax.ShapeDtypeStruct((split_rows, D_MODEL), F32),
                     jax.ShapeDtypeStruct((n - split_rows, D_MODEL), F32)]
    return pl.pallas_call(
        functools.partial(_gather_ln_kernel, split=split),
        grid_spec=pltpu.PrefetchScalarGridSpec(
            num_scalar_prefetch=1, grid=(n // tm,),
            in_specs=[row, pl.BlockSpec(memory_space=pl.ANY), vec, vec],
            out_specs=out_specs,
            scratch_shapes=[pltpu.VMEM((tm, D_MODEL), F32), pltpu.SemaphoreType.DMA(())],
        ),
        out_shape=out_shape,
        compiler_params=_params(("arbitrary",)), name="gather_ln",
    )(dest, h, ys, g.reshape(1, -1), b.reshape(1, -1))


def _gelu_tanh(x):
    return 0.5 * x * (1.0 + jnp.tanh(0.7978845608028654 * (x + 0.044715 * x * x * x)))


def _moe_kernel(ea_ref, eb_ref, used_ref, x_ref, w1a_ref, w2a_ref, w1b_ref, w2b_ref, o_ref):
    @pl.when(pl.program_id(0) < used_ref[0])
    def _():
        x = x_ref[:, 0:D_MODEL].astype(BF16)

        def expert(w1_ref, w2_ref):
            hid = _gelu_tanh(jnp.dot(x, w1_ref[...], preferred_element_type=F32))
            return jnp.dot(hid.astype(BF16), w2_ref[...], preferred_element_type=F32)

        g = x_ref[:, D_MODEL:ROW_WIDTH]
        o_ref[...] = g[:, 0:1] * expert(w1a_ref, w2a_ref) + g[:, 1:2] * expert(w1b_ref, w2b_ref)

    @pl.when(pl.program_id(0) >= used_ref[0])
    def _():
        o_ref[...] = jnp.zeros_like(o_ref)


def _moe_call(xs, blk_ea, blk_eb, n_used, w1, w2, tm):
    npad = xs.shape[0]
    w1a = pl.BlockSpec((None, D_MODEL, D_EXPERT), lambda i, ea, eb, u: (ea[i], 0, 0))
    w2a = pl.BlockSpec((None, D_EXPERT, D_MODEL), lambda i, ea, eb, u: (ea[i], 0, 0))
    w1b = pl.BlockSpec((None, D_MODEL, D_EXPERT), lambda i, ea, eb, u: (eb[i], 0, 0))
    w2b = pl.BlockSpec((None, D_EXPERT, D_MODEL), lambda i, ea, eb, u: (eb[i], 0, 0))
    return pl.pallas_call(
        _moe_kernel,
        grid_spec=pltpu.PrefetchScalarGridSpec(
            num_scalar_prefetch=3, grid=(npad // tm,),
            in_specs=[pl.BlockSpec((tm, ROW_WIDTH), lambda i, *_: (i, 0)), w1a, w2a, w1b, w2b],
            out_specs=pl.BlockSpec((tm, D_MODEL), lambda i, *_: (i, 0)),
        ),
        out_shape=jax.ShapeDtypeStruct((npad, D_MODEL), F32),
        compiler_params=_params(("arbitrary",), VMEM_LIMIT), name="moe",
    )(blk_ea, blk_eb, n_used, xs, w1, w2, w1, w2)


def _routing_tables(cls, rank, counts, tm):
    nblk = cls.shape[0] // tm + N_CLASSES
    padded = (counts + tm - 1) // tm * tm
    pad_end = jnp.cumsum(padded)
    classes = jnp.arange(N_CLASSES, dtype=jnp.int32)
    dest = rank + jnp.sum(jnp.where(cls[:, None] == classes, pad_end - padded, 0), axis=1)
    n_used = pad_end[-1] // tm
    first_row = jnp.minimum(jnp.arange(nblk, dtype=jnp.int32), n_used - 1) * tm
    blk_cls = jnp.minimum(jnp.sum(first_row[:, None] >= pad_end, axis=1).astype(jnp.int32), N_CLASSES - 1)
    grp, pr = blk_cls // len(PAIRS), blk_cls % len(PAIRS)
    lo = (pr >= 3).astype(jnp.int32) + (pr >= 5).astype(jnp.int32)
    hi = jnp.where(pr < 3, pr + 1, jnp.where(pr < 5, pr - 1, 3))
    blk_ea = grp * EXPERTS_PER_GROUP + lo
    blk_eb = grp * EXPERTS_PER_GROUP + hi
    return dest, nblk * tm, blk_ea, blk_eb, n_used.reshape(1).astype(jnp.int32)


def _block_flags(seq_lens, t):
    first, last = [], []
    for length in seq_lens:
        nb = length // t
        first += [1] + [0] * (nb - 1)
        last += [0] * (nb - 1) + [1]
    return jnp.asarray(np.array(first, np.int32)), jnp.asarray(np.array(last, np.int32))


def _trunk(xa, xb, seq_lens, p, *, tm, tm_proj, t_conv, t_gla, tm_moe):
    (ln_in_g, ln_in_b, w_in, dw_w, dw_b, conv_ln_g, conv_ln_b, w_conv_out, hg_lower, hg_norm_g, w_hg_out,
     w_o, ln1_g, ln1_b, w_router, b_router, w_e1, w_e2, ln2_g, ln2_b) = p
    depth = w_in.shape[0]
    conv_first, conv_last = _block_flags(seq_lens, t_conv)
    gla_first, gla_last = _block_flags(seq_lens, t_gla)

    lb_p = jax.nn.softmax(hg_lower.astype(F32), axis=0)
    lb_all = jnp.clip(jnp.cumsum(lb_p, axis=0) - lb_p[0:1], 0.0, 1.0)
    w_in_b = w_in.reshape(depth, D_MODEL, 9, D_MODEL).transpose(0, 2, 1, 3).astype(BF16)
    w_router_t = w_router.T.astype(BF16)
    b_router_c = b_router.reshape(N_EXPERTS, 1).astype(F32)

    h = _ln_call(xa, xb, ln_in_g, ln_in_b, tm)
    for l in range(depth):
        u, qvo, f, gab = _proj_call(h, w_in_b[l], lb_all[l].reshape(2, 1, D_MODEL), tm_proj)
        ua = _conv_call(u, conv_first, conv_last, dw_w[l], dw_b[l], conv_ln_g[l], conv_ln_b[l], t_conv)
        o_bw = _gla_call(gla_last, qvo, f, 1, t_gla)
        ob = _gla_call(gla_first, qvo, f, 0, t_gla, obw=o_bw, g=hg_norm_g[l].astype(F32))
        h1 = _merge_call(h, ua, ob, gab, w_conv_out[l].astype(BF16), w_hg_out[l].astype(BF16),
                         w_o[l].astype(BF16), ln1_g[l], ln1_b[l], tm)
        cls, rank, gates, counts = _router_call(h1, w_router_t, b_router_c, tm)
        dest, n_rows, blk_ea, blk_eb, n_used = _routing_tables(
            cls[0], rank[0], counts[:N_CLASSES, 0].astype(jnp.int32), tm_moe)
        xs = _scatter_call(h1, gates, dest, n_rows, tm)
        ys = _moe_call(xs, blk_ea, blk_eb, n_used, w_e1[l].astype(BF16), w_e2[l].astype(BF16), tm_moe)
        h = _gather_ln_call(h1, ys, dest, ln2_g[l], ln2_b[l], tm,
                            split_rows=xa.shape[0] if l == depth - 1 else None)
    return h


def kernel(x_prompt, x_sample, ln_in_g, ln_in_b, w_in, dw_w, dw_b, conv_ln_g, conv_ln_b, w_conv_out, hg_lower,
           hg_norm_g, w_hg_out, w_o, ln1_g, ln1_b, w_router, b_router, w_e1, w_e2, ln2_g, ln2_b):
    bp, lp, d = x_prompt.shape
    bs, ls, _ = x_sample.shape
    seq_lens = (lp,) * bp + (ls,) * bs
    p = (ln_in_g, ln_in_b, w_in, dw_w, dw_b, conv_ln_g, conv_ln_b, w_conv_out, hg_lower, hg_norm_g, w_hg_out,
         w_o, ln1_g, ln1_b, w_router, b_router, w_e1, w_e2, ln2_g, ln2_b)
    yp, ys = _trunk(x_prompt.reshape(bp * lp, d), x_sample.reshape(bs * ls, d), seq_lens, p,
                    tm=512, tm_proj=1024, t_conv=512, t_gla=2048, tm_moe=256)
    return (yp.reshape(bp, lp, d), ys.reshape(bs, ls, d))
```

```python
import functools
import math

import numpy as np
import jax
import jax.numpy as jnp
from jax import lax
from jax.experimental import pallas as pl
from jax.experimental.pallas import tpu as pltpu

F32 = jnp.float32
BF16 = jnp.bfloat16

D_MODEL = 1024
DEPTH = 4
HEAD_DIM = 128
N_HEADS = D_MODEL // HEAD_DIM
CONV_WIDTH = 31
CONV_PAD = CONV_WIDTH // 2
HALO = 16
CHUNK = 64
GLA_GROUP = 4
EXP_CLAMP = 60.0
N_EXPERTS = 16
N_GROUPS = 4
EXPERTS_PER_GROUP = 4
D_EXPERT = 2 * D_MODEL
PAIRS = ((0, 1), (0, 2), (0, 3), (1, 2), (1, 3), (2, 3))
N_CLASSES = N_GROUPS * len(PAIRS)
ALPHA = (2 * DEPTH) ** 0.25
LN_EPS = 1e-5
RMS_EPS = 1e-6
VMEM_LIMIT = 56 * 1024 * 1024


def _params(sem, vmem=None):
    return pltpu.CompilerParams(dimension_semantics=sem, vmem_limit_bytes=vmem)


def _sigmoid(x):
    return 1.0 / (1.0 + jnp.exp(-x))


def _layer_norm_rows(x, g, b):
    mu = jnp.mean(x, axis=-1, keepdims=True)
    xc = x - mu
    var = jnp.mean(xc * xc, axis=-1, keepdims=True)
    return xc * lax.rsqrt(var + LN_EPS) * g + b


def _ln_kernel(xa_ref, xb_ref, g_ref, b_ref, o_ref, *, split):
    i = pl.program_id(0)

    @pl.when(i < split)
    def _():
        o_ref[...] = _layer_norm_rows(xa_ref[...], g_ref[...], b_ref[...])

    @pl.when(i >= split)
    def _():
        o_ref[...] = _layer_norm_rows(xb_ref[...], g_ref[...], b_ref[...])


def _ln_call(xa, xb, g, b, tm):
    na, nb = xa.shape[0], xb.shape[0]
    split = na // tm
    vec = pl.BlockSpec((1, D_MODEL), lambda i: (0, 0))
    return pl.pallas_call(
        functools.partial(_ln_kernel, split=split), grid=((na + nb) // tm,),
        in_specs=[pl.BlockSpec((tm, D_MODEL), lambda i: (jnp.minimum(i, split - 1), 0)),
                  pl.BlockSpec((tm, D_MODEL), lambda i: (jnp.maximum(i - split, 0), 0)), vec, vec],
        out_specs=pl.BlockSpec((tm, D_MODEL), lambda i: (i, 0)),
        out_shape=jax.ShapeDtypeStruct((na + nb, D_MODEL), F32),
        compiler_params=_params(("arbitrary",)), name="ln_in",
    )(xa, xb, g.reshape(1, -1), b.reshape(1, -1))


PROJ_COLS = 256


def _proj_kernel(x_ref, wv_ref, w_ref, lb_ref, u_ref, qvo_ref, f_ref, gab_ref, xb_ref):
    j = pl.program_id(1)

    @pl.when(j == 0)
    def _():
        xb_ref[...] = x_ref[...].astype(BF16)

    x = xb_ref[...]
    pieces = [slice(c * PROJ_COLS, (c + 1) * PROJ_COLS) for c in range(D_MODEL // PROJ_COLS)]

    def mm(ref, cols):
        return jnp.dot(x, ref[:, cols], preferred_element_type=F32)

    def heads_store(ref, cols, val):
        for k in range(PROJ_COLS // HEAD_DIM):
            ref[cols.start // HEAD_DIM + k] = val[:, k * HEAD_DIM:(k + 1) * HEAD_DIM].astype(ref.dtype)

    @pl.when(j == 0)
    def _():
        for cols in pieces:
            u_ref[:, cols] = (mm(wv_ref, cols) * _sigmoid(mm(w_ref, cols))).astype(BF16)

    @pl.when(j == 1)
    def _():
        for cols in pieces:
            a = mm(w_ref, cols)
            heads_store(qvo_ref, cols, a * _sigmoid(a) * HEAD_DIM ** -0.5)

    @pl.when(j == 2)
    def _():
        for cols in pieces:
            heads_store(qvo_ref, cols, mm(w_ref, cols))

    @pl.when(j == 3)
    def _():
        for cols in pieces:
            a = mm(w_ref, cols)
            heads_store(qvo_ref, cols, a * _sigmoid(a))

    @pl.when((j == 4) | (j == 5))
    def _():
        for cols in pieces:
            t = jnp.exp(-mm(w_ref, cols))
            f = (1.0 + lb_ref[:, cols] * jnp.minimum(t, math.exp(EXP_CLAMP))) / (1.0 + t)
            heads_store(f_ref, cols, jnp.minimum(f, 1.0))

    @pl.when(j >= 6)
    def _():
        for cols in pieces:
            gab_ref[:, cols] = _sigmoid(mm(w_ref, cols)).astype(BF16)


def _proj_group(j):
    t = j + 1
    return jnp.where((t == 3) | (t == 4), t + 2, jnp.where((t == 5) | (t == 6), t - 2, t))


def _proj_call(h, w, lb, layer, tm):
    n = h.shape[0]
    return pl.pallas_call(
        _proj_kernel, grid=(n // tm, 8),
        in_specs=[
            pl.BlockSpec((tm, D_MODEL), lambda i, j: (i, 0)),
            pl.BlockSpec((None, None, D_MODEL, D_MODEL), lambda i, j: (layer, 0, 0, 0)),
            pl.BlockSpec((None, None, D_MODEL, D_MODEL), lambda i, j: (layer, _proj_group(j), 0, 0)),
            pl.BlockSpec((None, 1, D_MODEL), lambda i, j: (jnp.clip(j - 4, 0, 1), 0, 0)),
        ],
        out_specs=[
            pl.BlockSpec((tm, D_MODEL), lambda i, j: (i, 0)),
            pl.BlockSpec((None, N_HEADS, tm, HEAD_DIM), lambda i, j: (jnp.clip(j - 1, 0, 2), 0, i, 0)),
            pl.BlockSpec((None, N_HEADS, tm, HEAD_DIM), lambda i, j: (jnp.clip(j - 4, 0, 1), 0, i, 0)),
            pl.BlockSpec((None, tm, D_MODEL), lambda i, j: (jnp.clip(j - 6, 0, 1), i, 0)),
        ],
        out_shape=[
            jax.ShapeDtypeStruct((n, D_MODEL), BF16),
            jax.ShapeDtypeStruct((3, N_HEADS, n, HEAD_DIM), BF16),
            jax.ShapeDtypeStruct((2, N_HEADS, n, HEAD_DIM), F32),
            jax.ShapeDtypeStruct((2, n, D_MODEL), BF16),
        ],
        scratch_shapes=[pltpu.VMEM((tm, D_MODEL), BF16)],
        compiler_params=_params(("parallel", "arbitrary"), VMEM_LIMIT), name="proj",
    )(h, w, w, lb)


def _conv_kernel(first_ref, last_ref, up_ref, u_ref, un_ref, w_ref, b_ref, g_ref, beta_ref, o_ref,
                 buf_ref, acc_ref, *, rows):
    i = pl.program_id(0)
    t = u_ref.shape[0]
    nslab = D_MODEL // 128
    prev = jnp.where(first_ref[i] == 1, 0.0, up_ref[...].astype(F32))
    nxt = jnp.where(last_ref[i] == 1, 0.0, un_ref[...].astype(F32))
    for c in range(nslab):
        cols = slice(c * 128, (c + 1) * 128)
        buf_ref[c, 0:HALO, :] = prev[:, cols]
        buf_ref[c, HALO:HALO + t, :] = u_ref[:, cols].astype(F32)
        buf_ref[c, HALO + t:2 * HALO + t, :] = nxt[:, cols]

    def slab(c, carry):
        for rb in range(t // rows):
            acc = jnp.zeros((rows, 128), F32)
            for k in range(CONV_WIDTH):
                r0 = rb * rows + HALO - CONV_PAD + k
                acc = acc + w_ref[c, k:k + 1, :] * buf_ref[c, r0:r0 + rows, :]
            acc_ref[c, rb * rows:(rb + 1) * rows, :] = acc
        return carry

    lax.fori_loop(0, nslab, slab, 0)
    x = jnp.concatenate([acc_ref[c] for c in range(nslab)], axis=-1) + b_ref[...]
    y = _layer_norm_rows(x, g_ref[...], beta_ref[...])
    o_ref[...] = (y * _sigmoid(y)).astype(BF16)


def _conv_call(u, first, last, w, b, g, beta, t):
    n = u.shape[0]
    nb, hb = n // t, t // HALO
    nslab = D_MODEL // 128
    wpad = jnp.zeros((32, D_MODEL), F32).at[:CONV_WIDTH].set(w).reshape(32, nslab, 128).transpose(1, 0, 2)
    vec = pl.BlockSpec((1, D_MODEL), lambda i, *_: (0, 0))
    return pl.pallas_call(
        functools.partial(_conv_kernel, rows=32),
        grid_spec=pltpu.PrefetchScalarGridSpec(
            num_scalar_prefetch=2, grid=(nb,),
            in_specs=[
                pl.BlockSpec((HALO, D_MODEL), lambda i, *_: (jnp.maximum(i * hb - 1, 0), 0)),
                pl.BlockSpec((t, D_MODEL), lambda i, *_: (i, 0)),
                pl.BlockSpec((HALO, D_MODEL), lambda i, *_: (jnp.minimum((i + 1) * hb, nb * hb - 1), 0)),
                pl.BlockSpec((nslab, 32, 128), lambda i, *_: (0, 0, 0)),
                vec, vec, vec,
            ],
            out_specs=pl.BlockSpec((t, D_MODEL), lambda i, *_: (i, 0)),
            scratch_shapes=[pltpu.VMEM((nslab, t + 2 * HALO, 128), F32), pltpu.VMEM((nslab, t, 128), F32)],
        ),
        out_shape=jax.ShapeDtypeStruct((n, D_MODEL), BF16),
        compiler_params=_params(("parallel",), VMEM_LIMIT), name="conv",
    )(first, last, u, u, u, wpad, b.reshape(1, -1), g.reshape(1, -1), beta.reshape(1, -1))


def _mul(a, b):
    if a is None:
        return b
    if b is None:
        return a
    return a * b


def _cat(tiles):
    return jnp.concatenate(tiles, axis=0)


def _nt(a, b):
    return lax.dot_general(a, b, (((1,), (1,)), ((), ())), preferred_element_type=F32)


def _tn(a, b):
    return lax.dot_general(a, b, (((0,), (0,)), ((), ())), preferred_element_type=F32)


def _gla_masks():
    i = lax.broadcasted_iota(jnp.int32, (CHUNK, CHUNK), 0)
    j = lax.broadcasted_iota(jnp.int32, (CHUNK, CHUNK), 1)
    rt, rs = i & 7, j & 7
    return [(rt >> 1) == (rs >> 1), (rt >> 2) == (rs >> 2), None]


def _block_products(tot, rev):
    ri = lax.broadcasted_iota(jnp.int32, tot.shape, 0)
    r = (7 - ri) if rev else ri

    def sh(x, d):
        return pltpu.roll(x, ((-d) if rev else d) % 8, 0)

    one = jnp.ones_like(tot)
    zero = jnp.zeros_like(tot)
    b0, b1, b2 = (r & 1) == 1, (r & 2) == 2, (r & 4) == 4
    m4 = r & 3
    d1, u1 = sh(tot, 1), sh(tot, -1)

    def excl_prefix(width):
        m = r & (width - 1)
        e = jnp.where(m >= 1, d1, one)
        step = 1
        while step < width:
            e = e * jnp.where(m >= step, sh(e, step), one)
            step *= 2
        return e

    def excl_suffix(width):
        m = r & (width - 1)
        e = jnp.where(m <= width - 2, u1, one)
        step = 1
        while step < width:
            e = e * jnp.where(m <= width - 1 - step, sh(e, -step), one)
            step *= 2
        return e

    lvl3 = (jnp.where(b0, one, zero), jnp.where(b0, zero, one))
    lvl4 = (jnp.where(b1, jnp.where(b0, d1, one), zero), jnp.where(b1, zero, jnp.where(b0, one, u1)))
    lvl5 = (jnp.where(b2, excl_prefix(4), zero), jnp.where(b2, zero, excl_suffix(4)))
    state = (excl_prefix(8), excl_suffix(8))
    del m4
    return [lvl3, lvl4, lvl5, state]


def _gla_prep(fs, qs, vs, rev):
    n = 8
    ks = [1.0 - f for f in fs]
    fi = [list(fs)]
    fi.append([fs[a] * fs[a - 1] if a & 1 else fs[a] for a in range(n)])
    fi.append([fi[1][a] * fi[1][(a & ~3) + 1] if a & 2 else fi[1][a] for a in range(n)])
    fi.append([fi[2][a] * fi[2][3] if a & 4 else fi[2][a] for a in range(n)])
    ge = [[None] * n]
    ge.append([fs[a + 1] if not a & 1 else None for a in range(n)])
    ge.append([_mul(ge[1][a], fi[1][(a & ~3) + 3]) if not a & 2 else ge[1][a] for a in range(n)])
    ge.append([_mul(ge[2][a], fi[2][7]) if not a & 4 else ge[2][a] for a in range(n)])

    low = [jnp.sum(qs[a] * ks[a], axis=-1, keepdims=True) * vs[a] for a in range(n)]
    for lvl in range(3):
        half = 1 << lvl
        kt = [None if (a >> lvl) & 1 else _mul(ks[a], ge[lvl][a]) for a in range(n)]
        for a in range(n):
            if not (a >> lvl) & 1:
                continue
            qt = qs[a] * fi[lvl][a]
            first = a & ~(2 * half - 1)
            for s in range(first, first + half):
                low[a] = low[a] + jnp.sum(qt * kt[s], axis=-1, keepdims=True) * vs[s]
    low = _cat(low)

    levels = []
    qf = [qs[a] * fi[3][a] for a in range(n)]
    kg = [_mul(ks[a], ge[3][a]) for a in range(n)]
    tot = fi[3][7]
    facs = _block_products(tot, rev)
    for lvl in range(3):
        rq, rk = facs[lvl]
        levels.append((_cat([x * rq for x in qf]).astype(BF16), _cat([x * rk for x in kg]).astype(BF16)))
    rq, rk = facs[3]
    q_in = _cat([x * rq for x in qf]).astype(BF16)
    k_out = _cat([x * rk for x in kg]).astype(BF16)
    last = 0 if rev else 7
    decay = (rq * tot)[last:last + 1, :]
    return levels, (_cat(vs).astype(BF16), q_in, k_out, decay, low)


def _gla_kernel(flag_ref, q_ref, v_ref, f_ref, *rest, rev, final):
    if final:
        obw_ref, og_ref, g_ref, o_ref, st_ref, qs_ref, vs_ref, os_ref = rest
    else:
        o_ref, st_ref, qs_ref, vs_ref = rest
        os_ref = o_ref
    nb = pl.num_programs(1)
    tb = pl.program_id(1)
    blk = (nb - 1 - tb) if rev else tb
    t = q_ref.shape[0]
    nchunks = t // CHUNK

    @pl.when(flag_ref[blk] == 1)
    def _():
        st_ref[...] = jnp.zeros_like(st_ref)

    qs_ref[...] = q_ref[...].astype(F32)
    vs_ref[...] = v_ref[...].astype(F32)
    masks = _gla_masks()
    order = list(range(7, -1, -1)) if rev else list(range(8))

    def rows_of(ci):
        c0 = ((nchunks - 1 - ci) if rev else ci) * CHUNK
        if not isinstance(c0, int):
            c0 = pl.multiple_of(c0, CHUNK)
        return [pl.ds(c0 + a, 8, stride=8) for a in order]

    def prep(ci):
        rows = rows_of(ci)
        return _gla_prep([f_ref[rw, :] for rw in rows], [qs_ref[rw, :] for rw in rows],
                         [vs_ref[rw, :] for rw in rows], rev)

    grp = min(GLA_GROUP, nchunks)
    ngroups = nchunks // grp

    def operands(g):
        return [prep(g * grp + u) for u in range(grp)]

    def products(ops):
        prods = []
        for levels, (vb, q_in, k_out, decay, low) in ops:
            att = None
            for (qt, kt), mask in zip(levels, masks):
                part = _nt(qt, kt)
                if mask is not None:
                    part = jnp.where(mask, part, 0.0)
                att = part if att is None else att + part
            prods.append((att.astype(BF16), vb, q_in, _tn(vb, k_out), decay, low))
        return prods

    def finish(g, prods, st):
        for u, (att, vb, q_in, kv, decay, low) in enumerate(prods):
            out = jnp.dot(att, vb, preferred_element_type=F32) + _nt(q_in, st.astype(BF16)) + low
            st = st * decay + kv
            for a, rw in enumerate(rows_of(g * grp + u)):
                os_ref[rw, :] = out[8 * a:8 * a + 8]
        return st

    st = st_ref[...]
    ops = [operands(g) for g in range(min(2, ngroups))]
    prods = products(ops.pop(0))
    for g in range(ngroups):
        st = finish(g, prods, st)
        if ops:
            prods = products(ops.pop(0))
        if g + 2 < ngroups:
            ops.append(operands(g + 2))
    st_ref[...] = st

    if final:
        o = os_ref[...] + obw_ref[...]
        o = o * lax.rsqrt(jnp.mean(o * o, axis=-1, keepdims=True) + RMS_EPS) * g_ref[...]
        o_ref[...] = (o * og_ref[...].astype(F32)).astype(BF16)


def _gla_call(flags, qvo, f, direction, t, obw=None, g=None):
    n = qvo.shape[2]
    nb = n // t
    rev = direction == 1
    final = not rev

    def tmap(hd, tb, *_):
        return (nb - 1 - tb) if rev else tb

    def sel(k):
        return pl.BlockSpec((None, None, t, HEAD_DIM), lambda hd, tb, *_: (k, hd, tmap(hd, tb), 0))

    in_specs = [sel(0), sel(1), sel(direction)]
    args = [qvo, qvo, f]
    scratch = [pltpu.VMEM((HEAD_DIM, HEAD_DIM), F32), pltpu.VMEM((t, HEAD_DIM), F32),
               pltpu.VMEM((t, HEAD_DIM), F32)]
    if final:
        in_specs += [pl.BlockSpec((None, t, HEAD_DIM), lambda hd, tb, *_: (hd, tb, 0)), sel(2),
                     pl.BlockSpec((None, 1, HEAD_DIM), lambda hd, tb, *_: (hd, 0, 0))]
        args += [obw, qvo, g.reshape(N_HEADS, 1, HEAD_DIM)]
        scratch.append(pltpu.VMEM((t, HEAD_DIM), F32))
    return pl.pallas_call(
        functools.partial(_gla_kernel, rev=rev, final=final),
        grid_spec=pltpu.PrefetchScalarGridSpec(
            num_scalar_prefetch=1, grid=(N_HEADS, nb), in_specs=in_specs,
            out_specs=pl.BlockSpec((None, t, HEAD_DIM), lambda hd, tb, *_: (hd, tmap(hd, tb), 0)),
            scratch_shapes=scratch,
        ),
        out_shape=jax.ShapeDtypeStruct((N_HEADS, n, HEAD_DIM), BF16 if final else F32),
        compiler_params=_params(("parallel", "arbitrary"), VMEM_LIMIT),
        name="gla_fwd" if final else "gla_bwd",
    )(flags, *args)


def _merge_kernel(h_ref, ua_ref, ob_ref, gab_ref, wc_ref, whg_ref, wo_ref, g_ref, b_ref, o_ref):
    ob = jnp.concatenate([ob_ref[hd] for hd in range(N_HEADS)], axis=-1)
    branch_a = jnp.dot(ua_ref[...], wc_ref[...], preferred_element_type=F32)
    branch_b = jnp.dot(ob, whg_ref[...], preferred_element_type=F32)
    merged = gab_ref[0].astype(F32) * branch_a + gab_ref[1].astype(F32) * branch_b
    mix = jnp.dot(merged.astype(BF16), wo_ref[...], preferred_element_type=F32)
    o_ref[...] = _layer_norm_rows(ALPHA * h_ref[...] + mix, g_ref[...], b_ref[...])


def _merge_call(h, ua, ob, gab, wc, whg, wo, g, b, layer, tm):
    n = h.shape[0]
    row = pl.BlockSpec((tm, D_MODEL), lambda i: (i, 0))
    mat = pl.BlockSpec((None, D_MODEL, D_MODEL), lambda i: (layer, 0, 0))
    vec = pl.BlockSpec((1, D_MODEL), lambda i: (0, 0))
    return pl.pallas_call(
        _merge_kernel, grid=(n // tm,),
        in_specs=[row, row, pl.BlockSpec((N_HEADS, tm, HEAD_DIM), lambda i: (0, i, 0)),
                  pl.BlockSpec((2, tm, D_MODEL), lambda i: (0, i, 0)), mat, mat, mat, vec, vec],
        out_specs=row, out_shape=jax.ShapeDtypeStruct((n, D_MODEL), F32),
        compiler_params=_params(("parallel",), VMEM_LIMIT), name="merge",
    )(h, ua, ob, gab, wc, whg, wo, g.reshape(1, -1), b.reshape(1, -1))


GATE_LANES = 128
CLASS_ROWS = 32


def _router_kernel(h_ref, w_ref, b_ref, cls_ref, rank_ref, gate_ref, count_ref):
    @pl.when(pl.program_id(0) == 0)
    def _():
        count_ref[...] = jnp.zeros_like(count_ref)

    logits = lax.dot_general(w_ref[...], h_ref[...].astype(BF16), (((1,), (1,)), ((), ())),
                             preferred_element_type=F32) + b_ref[...]
    rows = [logits[e:e + 1, :] for e in range(N_EXPERTS)]
    mx = functools.reduce(jnp.maximum, rows)
    ex = [jnp.exp(x - mx) for x in rows]
    den = functools.reduce(lambda a, b: a + b, ex)
    sc = [e / den for e in ex]
    gscore = []
    for grp in range(N_GROUPS):
        s = sc[grp * 4:grp * 4 + 4]
        gscore.append(functools.reduce(jnp.maximum, [s[a] + s[b] for a, b in PAIRS]))
    gsel = jnp.zeros_like(gscore[0], dtype=jnp.int32)
    best = gscore[0]
    for grp in range(1, N_GROUPS):
        better = gscore[grp] > best
        gsel = jnp.where(better, grp, gsel)
        best = jnp.where(better, gscore[grp], best)
    ing = []
    for k in range(EXPERTS_PER_GROUP):
        x = sc[k]
        for grp in range(1, N_GROUPS):
            x = jnp.where(gsel == grp, sc[grp * 4 + k], x)
        ing.append(x)
    i1 = jnp.zeros_like(gsel)
    w1 = ing[0]
    for k in range(1, 4):
        better = ing[k] > w1
        i1 = jnp.where(better, k, i1)
        w1 = jnp.where(better, ing[k], w1)
    i2 = jnp.full_like(gsel, -1)
    w2 = jnp.full_like(w1, -1.0)
    for k in range(4):
        better = (i1 != k) & (ing[k] > w2)
        i2 = jnp.where(better, k, i2)
        w2 = jnp.where(better, ing[k], w2)
    tot = w1 + w2
    g1, g2 = w1 / tot, w2 / tot
    lo, hi = jnp.minimum(i1, i2), jnp.maximum(i1, i2)
    pair = jnp.where(lo == 0, hi - 1, jnp.where(lo == 1, hi + 1, 5))
    cls = gsel * len(PAIRS) + pair
    cls_ref[...] = cls
    tm = cls.shape[1]

    onehot = lax.broadcasted_iota(jnp.int32, (CLASS_ROWS, tm), 0) == cls
    upper = (lax.broadcasted_iota(jnp.int32, (tm, tm), 0) <= lax.broadcasted_iota(jnp.int32, (tm, tm), 1))
    prefix = jnp.dot(jnp.where(onehot, 1.0, 0.0).astype(BF16), jnp.where(upper, 1.0, 0.0).astype(BF16),
                     preferred_element_type=F32)
    seen = count_ref[:, 0:1]
    rank_ref[...] = jnp.sum(jnp.where(onehot, prefix + seen - 1.0, 0.0), axis=0, keepdims=True).astype(jnp.int32)
    count_ref[...] = count_ref[...] + prefix[:, tm - 1:tm]

    first_is_lo = i1 < i2
    gates = (jnp.where(first_is_lo, g1, g2), jnp.where(first_is_lo, g2, g1))
    rid = lax.broadcasted_iota(jnp.int32, (16, tm), 0)
    pieces = jnp.zeros((16, tm), F32)
    rest = list(gates)
    for part in range(3):
        for k in range(2):
            piece = rest[k].astype(BF16).astype(F32)
            rest[k] = rest[k] - piece
            pieces = jnp.where(rid == 2 * part + k, piece, pieces)
    lane = lax.broadcasted_iota(jnp.int32, (16, GATE_LANES), 1)
    row = lax.broadcasted_iota(jnp.int32, (16, GATE_LANES), 0)
    place = jnp.where((row < 6) & (lane == (row & 1)), 1.0, 0.0).astype(BF16)
    gate_ref[...] = _tn(pieces.astype(BF16), place)


def _router_call(h, w_t, b, tm):
    n = h.shape[0]
    return pl.pallas_call(
        _router_kernel, grid=(n // tm,),
        in_specs=[pl.BlockSpec((tm, D_MODEL), lambda i: (i, 0)),
                  pl.BlockSpec((N_EXPERTS, D_MODEL), lambda i: (0, 0)),
                  pl.BlockSpec((N_EXPERTS, 1), lambda i: (0, 0))],
        out_specs=[pl.BlockSpec((1, tm), lambda i: (0, i)), pl.BlockSpec((1, tm), lambda i: (0, i)),
                   pl.BlockSpec((tm, GATE_LANES), lambda i: (i, 0)),
                   pl.BlockSpec((CLASS_ROWS, 128), lambda i: (0, 0))],
        out_shape=[jax.ShapeDtypeStruct((1, n), jnp.int32), jax.ShapeDtypeStruct((1, n), jnp.int32),
                   jax.ShapeDtypeStruct((n, GATE_LANES), F32), jax.ShapeDtypeStruct((CLASS_ROWS, 128), F32)],
        compiler_params=_params(("arbitrary",)), name="router",
    )(h, w_t, b)


ROW_WIDTH = D_MODEL + GATE_LANES
DMA_UNROLL = 8


def _scatter_kernel(dest_ref, h_ref, g_ref, init_ref, o_ref, row_ref, sem):
    del init_ref
    tm = h_ref.shape[0]
    base = pl.program_id(0) * tm
    row_ref[:, 0:D_MODEL] = h_ref[...]
    row_ref[:, D_MODEL:ROW_WIDTH] = g_ref[...]

    def start(t, carry):
        r0 = pl.multiple_of(t * 8, 8)
        for k in range(8):
            pltpu.make_async_copy(row_ref.at[pl.ds(r0 + k, 1), :],
                                  o_ref.at[pl.ds(dest_ref[base + r0 + k], 1), :], sem).start()
        return carry

    lax.fori_loop(0, tm // 8, start, 0)
    pltpu.make_async_copy(row_ref, o_ref.at[pl.ds(0, tm), :], sem).wait()


def _scatter_call(h, gates, dest, n_out, tm):
    n = h.shape[0]
    return pl.pallas_call(
        _scatter_kernel,
        grid_spec=pltpu.PrefetchScalarGridSpec(
            num_scalar_prefetch=1, grid=(n // tm,),
            in_specs=[pl.BlockSpec((tm, D_MODEL), lambda i, *_: (i, 0)),
                      pl.BlockSpec((tm, GATE_LANES), lambda i, *_: (i, 0)),
                      pl.BlockSpec(memory_space=pl.ANY)],
            out_specs=pl.BlockSpec(memory_space=pl.ANY),
            scratch_shapes=[pltpu.VMEM((tm, ROW_WIDTH), F32), pltpu.SemaphoreType.DMA(())],
        ),
        out_shape=jax.ShapeDtypeStruct((n_out, ROW_WIDTH), F32),
        input_output_aliases={3: 0},
        compiler_params=_params(("arbitrary",)), name="scatter",
    )(dest, h, gates, jnp.zeros((n_out, ROW_WIDTH), F32))


def _gather_ln_kernel(dest_ref, h_ref, y_ref, g_ref, b_ref, *rest, split):
    *o_refs, row_ref, sem = rest
    tm = h_ref.shape[0]
    i = pl.program_id(0)
    base = i * tm

    def start(r, carry):
        pltpu.make_async_copy(y_ref.at[pl.ds(dest_ref[base + r], 1), :], row_ref.at[pl.ds(r, 1), :], sem).start()
        return carry

    lax.fori_loop(0, tm, start, 0, unroll=DMA_UNROLL)
    pltpu.make_async_copy(y_ref.at[pl.ds(0, tm), :], row_ref, sem).wait()
    out = _layer_norm_rows(ALPHA * h_ref[...] + row_ref[...], g_ref[...], b_ref[...])
    if split is None:
        o_refs[0][...] = out
    else:
        @pl.when(i < split)
        def _():
            o_refs[0][...] = out

        @pl.when(i >= split)
        def _():
            o_refs[1][...] = out


def _gather_ln_call(h, ys, dest, g, b, tm, split_rows=None):
    n = h.shape[0]
    row = pl.BlockSpec((tm, D_MODEL), lambda i, *_: (i, 0))
    vec = pl.BlockSpec((1, D_MODEL), lambda i, *_: (0, 0))
    if split_rows is None:
        split, out_specs, out_shape = None, row, jax.ShapeDtypeStruct((n, D_MODEL), F32)
    else:
        split = split_rows // tm
        out_specs = [pl.BlockSpec((tm, D_MODEL), lambda i, *_: (jnp.minimum(i, split - 1), 0)),
                     pl.BlockSpec((tm, D_MODEL), lambda i, *_: (jnp.maximum(i - split, 0), 0))]
        out_shape = [jax.ShapeDtypeStruct((split_rows, D_MODEL), F32),
                     jax.ShapeDtypeStruct((n - split_rows, D_MODEL), F32)]
    return pl.pallas_call(
        functools.partial(_gather_ln_kernel, split=split),
        grid_spec=pltpu.PrefetchScalarGridSpec(
            num_scalar_prefetch=1, grid=(n // tm,),
            in_specs=[row, pl.BlockSpec(memory_space=pl.ANY), vec, vec],
            out_specs=out_specs,
            scratch_shapes=[pltpu.VMEM((tm, D_MODEL), F32), pltpu.SemaphoreType.DMA(())],
        ),
        out_shape=out_shape,
        compiler_params=_params(("arbitrary",)), name="gather_ln",
    )(dest, h, ys, g.reshape(1, -1), b.reshape(1, -1))


def _gelu_tanh(x):
    return 0.5 * x * (1.0 + jnp.tanh(0.7978845608028654 * (x + 0.044715 * x * x * x)))


def _moe_kernel(ea_ref, eb_ref, used_ref, x_ref, w1a_ref, w2a_ref, w1b_ref, w2b_ref, o_ref):
    @pl.when(pl.program_id(0) < used_ref[0])
    def _():
        x = x_ref[:, 0:D_MODEL].astype(BF16)

        def expert(w1_ref, w2_ref):
            hid = _gelu_tanh(jnp.dot(x, w1_ref[...], preferred_element_type=F32))
            return jnp.dot(hid.astype(BF16), w2_ref[...], preferred_element_type=F32)

        g = x_ref[:, D_MODEL:ROW_WIDTH]
        o_ref[...] = g[:, 0:1] * expert(w1a_ref, w2a_ref) + g[:, 1:2] * expert(w1b_ref, w2b_ref)

    @pl.when(pl.program_id(0) >= used_ref[0])
    def _():
        o_ref[...] = jnp.zeros_like(o_ref)


def _moe_call(xs, blk_ea, blk_eb, n_used, w1, w2, layer, tm):
    npad = xs.shape[0]
    w1a = pl.BlockSpec((None, None, D_MODEL, D_EXPERT), lambda i, ea, eb, u: (layer, ea[i], 0, 0))
    w2a = pl.BlockSpec((None, None, D_EXPERT, D_MODEL), lambda i, ea, eb, u: (layer, ea[i], 0, 0))
    w1b = pl.BlockSpec((None, None, D_MODEL, D_EXPERT), lambda i, ea, eb, u: (layer, eb[i], 0, 0))
    w2b = pl.BlockSpec((None, None, D_EXPERT, D_MODEL), lambda i, ea, eb, u: (layer, eb[i], 0, 0))
    return pl.pallas_call(
        _moe_kernel,
        grid_spec=pltpu.PrefetchScalarGridSpec(
            num_scalar_prefetch=3, grid=(npad // tm,),
            in_specs=[pl.BlockSpec((tm, ROW_WIDTH), lambda i, *_: (i, 0)), w1a, w2a, w1b, w2b],
            out_specs=pl.BlockSpec((tm, D_MODEL), lambda i, *_: (i, 0)),
        ),
        out_shape=jax.ShapeDtypeStruct((npad, D_MODEL), F32),
        compiler_params=_params(("arbitrary",), VMEM_LIMIT), name="moe",
    )(blk_ea, blk_eb, n_used, xs, w1, w2, w1, w2)


def _routing_tables(cls, rank, counts, tm):
    nblk = cls.shape[0] // tm + N_CLASSES
    padded = (counts + tm - 1) // tm * tm
    pad_end = jnp.cumsum(padded)
    classes = jnp.arange(N_CLASSES, dtype=jnp.int32)
    dest = rank + jnp.sum(jnp.where(cls[:, None] == classes, pad_end - padded, 0), axis=1)
    n_used = pad_end[-1] // tm
    first_row = jnp.minimum(jnp.arange(nblk, dtype=jnp.int32), n_used - 1) * tm
    blk_cls = jnp.minimum(jnp.sum(first_row[:, None] >= pad_end, axis=1).astype(jnp.int32), N_CLASSES - 1)
    grp, pr = blk_cls // len(PAIRS), blk_cls % len(PAIRS)
    lo = (pr >= 3).astype(jnp.int32) + (pr >= 5).astype(jnp.int32)
    hi = jnp.where(pr < 3, pr + 1, jnp.where(pr < 5, pr - 1, 3))
    blk_ea = grp * EXPERTS_PER_GROUP + lo
    blk_eb = grp * EXPERTS_PER_GROUP + hi
    return dest, nblk * tm, blk_ea, blk_eb, n_used.reshape(1).astype(jnp.int32)


def _block_flags(seq_lens, t):
    first, last = [], []
    for length in seq_lens:
        nb = length // t
        first += [1] + [0] * (nb - 1)
        last += [0] * (nb - 1) + [1]
    return jnp.asarray(np.array(first, np.int32)), jnp.asarray(np.array(last, np.int32))


def _trunk(xa, xb, seq_lens, p, *, tm, tm_proj, t_conv, t_gla, tm_moe):
    (ln_in_g, ln_in_b, w_in, dw_w, dw_b, conv_ln_g, conv_ln_b, w_conv_out, hg_lower, hg_norm_g, w_hg_out,
     w_o, ln1_g, ln1_b, w_router, b_router, w_e1, w_e2, ln2_g, ln2_b) = p
    depth = w_in.shape[0]
    conv_first, conv_last = _block_flags(seq_lens, t_conv)
    gla_first, gla_last = _block_flags(seq_lens, t_gla)

    lb_p = jax.nn.softmax(hg_lower.astype(F32), axis=0)
    lb_all = jnp.clip(jnp.cumsum(lb_p, axis=0) - lb_p[0:1], 0.0, 1.0)
    w_in_b = w_in.reshape(depth, D_MODEL, 9, D_MODEL).transpose(0, 2, 1, 3).astype(BF16)
    w_router_t = w_router.T.astype(BF16)
    b_router_c = b_router.reshape(N_EXPERTS, 1).astype(F32)
    wc_b, whg_b, wo_b = w_conv_out.astype(BF16), w_hg_out.astype(BF16), w_o.astype(BF16)
    w_e1_b, w_e2_b = w_e1.astype(BF16), w_e2.astype(BF16)

    h = _ln_call(xa, xb, ln_in_g, ln_in_b, tm)
    for l in range(depth):
        u, qvo, f, gab = _proj_call(h, w_in_b, lb_all[l].reshape(2, 1, D_MODEL), l, tm_proj)
        ua = _conv_call(u, conv_first, conv_last, dw_w[l], dw_b[l], conv_ln_g[l], conv_ln_b[l], t_conv)
        o_bw = _gla_call(gla_last, qvo, f, 1, t_gla)
        ob = _gla_call(gla_first, qvo, f, 0, t_gla, obw=o_bw, g=hg_norm_g[l].astype(F32))
        h1 = _merge_call(h, ua, ob, gab, wc_b, whg_b, wo_b, ln1_g[l], ln1_b[l], l, tm)
        cls, rank, gates, counts = _router_call(h1, w_router_t, b_router_c, tm)
        dest, n_rows, blk_ea, blk_eb, n_used = _routing_tables(
            cls[0], rank[0], counts[:N_CLASSES, 0].astype(jnp.int32), tm_moe)
        xs = _scatter_call(h1, gates, dest, n_rows, tm)
        ys = _moe_call(xs, blk_ea, blk_eb, n_used, w_e1_b, w_e2_b, l, tm_moe)
        h = _gather_ln_call(h1, ys, dest, ln2_g[l], ln2_b[l], tm,
                            split_rows=xa.shape[0] if l == depth - 1 else None)
    return h


def kernel(x_prompt, x_sample, ln_in_g, ln_in_b, w_in, dw_w, dw_b, conv_ln_g, conv_ln_b, w_conv_out, hg_lower,
           hg_norm_g, w_hg_out, w_o, ln1_g, ln1_b, w_router, b_router, w_e1, w_e2, ln2_g, ln2_b):
    bp, lp, d = x_prompt.shape
    bs, ls, _ = x_sample.shape
    seq_lens = (lp,) * bp + (ls,) * bs
    p = (ln_in_g, ln_in_b, w_in, dw_w, dw_b, conv_ln_g, conv_ln_b, w_conv_out, hg_lower, hg_norm_g, w_hg_out,
         w_o, ln1_g, ln1_b, w_router, b_router, w_e1, w_e2, ln2_g, ln2_b)
    yp, ys = _trunk(x_prompt.reshape(bp * lp, d), x_sample.reshape(bs * ls, d), seq_lens, p,
                    tm=512, tm_proj=1024, t_conv=512, t_gla=2048, tm_moe=256)
    return (yp.reshape(bp, lp, d), ys.reshape(bs, ls, d))
```

```python
import functools
import math

import numpy as np
import jax
import jax.numpy as jnp
from jax import lax
from jax.experimental import pallas as pl
from jax.experimental.pallas import tpu as pltpu

F32 = jnp.float32
BF16 = jnp.bfloat16

D_MODEL = 1024
DEPTH = 4
HEAD_DIM = 128
N_HEADS = D_MODEL // HEAD_DIM
CONV_WIDTH = 31
CONV_PAD = CONV_WIDTH // 2
HALO = 16
CHUNK = 64
GLA_GROUP = 4
EXP_CLAMP = 60.0
N_EXPERTS = 16
N_GROUPS = 4
EXPERTS_PER_GROUP = 4
D_EXPERT = 2 * D_MODEL
PAIRS = ((0, 1), (0, 2), (0, 3), (1, 2), (1, 3), (2, 3))
N_CLASSES = N_GROUPS * len(PAIRS)
ALPHA = (2 * DEPTH) ** 0.25
LN_EPS = 1e-5
RMS_EPS = 1e-6
VMEM_LIMIT = 56 * 1024 * 1024


def _params(sem, vmem=None):
    return pltpu.CompilerParams(dimension_semantics=sem, vmem_limit_bytes=vmem)


def _sigmoid(x):
    return 1.0 / (1.0 + jnp.exp(-x))


def _layer_norm_rows(x, g, b):
    mu = jnp.mean(x, axis=-1, keepdims=True)
    xc = x - mu
    var = jnp.mean(xc * xc, axis=-1, keepdims=True)
    return xc * lax.rsqrt(var + LN_EPS) * g + b


def _ln_kernel(xa_ref, xb_ref, g_ref, b_ref, o_ref, *, split):
    i = pl.program_id(0)

    @pl.when(i < split)
    def _():
        o_ref[...] = _layer_norm_rows(xa_ref[...], g_ref[...], b_ref[...])

    @pl.when(i >= split)
    def _():
        o_ref[...] = _layer_norm_rows(xb_ref[...], g_ref[...], b_ref[...])


def _ln_call(xa, xb, g, b, tm):
    na, nb = xa.shape[0], xb.shape[0]
    split = na // tm
    vec = pl.BlockSpec((1, D_MODEL), lambda i: (0, 0))
    return pl.pallas_call(
        functools.partial(_ln_kernel, split=split), grid=((na + nb) // tm,),
        in_specs=[pl.BlockSpec((tm, D_MODEL), lambda i: (jnp.minimum(i, split - 1), 0)),
                  pl.BlockSpec((tm, D_MODEL), lambda i: (jnp.maximum(i - split, 0), 0)), vec, vec],
        out_specs=pl.BlockSpec((tm, D_MODEL), lambda i: (i, 0)),
        out_shape=jax.ShapeDtypeStruct((na + nb, D_MODEL), F32),
        compiler_params=_params(("arbitrary",)), name="ln_in",
    )(xa, xb, g.reshape(1, -1), b.reshape(1, -1))


PROJ_COLS = 256


def _proj_kernel(x_ref, wv_ref, w_ref, lb_ref, u_ref, qvo_ref, f_ref, gab_ref, xb_ref):
    j = pl.program_id(1)

    @pl.when(j == 0)
    def _():
        xb_ref[...] = x_ref[...].astype(BF16)

    x = xb_ref[...]
    pieces = [slice(c * PROJ_COLS, (c + 1) * PROJ_COLS) for c in range(D_MODEL // PROJ_COLS)]

    def mm(ref, cols):
        return jnp.dot(x, ref[:, cols], preferred_element_type=F32)

    def heads_store(ref, cols, val):
        for k in range(PROJ_COLS // HEAD_DIM):
            ref[cols.start // HEAD_DIM + k] = val[:, k * HEAD_DIM:(k + 1) * HEAD_DIM].astype(ref.dtype)

    @pl.when(j == 0)
    def _():
        for cols in pieces:
            u_ref[:, cols] = (mm(wv_ref, cols) * _sigmoid(mm(w_ref, cols))).astype(BF16)

    @pl.when(j == 1)
    def _():
        for cols in pieces:
            a = mm(w_ref, cols)
            heads_store(qvo_ref, cols, a * _sigmoid(a) * HEAD_DIM ** -0.5)

    @pl.when(j == 2)
    def _():
        for cols in pieces:
            heads_store(qvo_ref, cols, mm(w_ref, cols))

    @pl.when(j == 3)
    def _():
        for cols in pieces:
            a = mm(w_ref, cols)
            heads_store(qvo_ref, cols, a * _sigmoid(a))

    @pl.when((j == 4) | (j == 5))
    def _():
        for cols in pieces:
            t = jnp.exp(-mm(w_ref, cols))
            f = (1.0 + lb_ref[:, cols] * jnp.minimum(t, math.exp(EXP_CLAMP))) / (1.0 + t)
            heads_store(f_ref, cols, jnp.minimum(f, 1.0))

    @pl.when(j >= 6)
    def _():
        for cols in pieces:
            gab_ref[:, cols] = _sigmoid(mm(w_ref, cols)).astype(BF16)


def _proj_group(j):
    t = j + 1
    return jnp.where((t == 3) | (t == 4), t + 2, jnp.where((t == 5) | (t == 6), t - 2, t))


def _proj_call(h, w, lb, layer, tm):
    n = h.shape[0]
    return pl.pallas_call(
        _proj_kernel, grid=(n // tm, 8),
        in_specs=[
            pl.BlockSpec((tm, D_MODEL), lambda i, j: (i, 0)),
            pl.BlockSpec((None, None, D_MODEL, D_MODEL), lambda i, j: (layer, 0, 0, 0)),
            pl.BlockSpec((None, None, D_MODEL, D_MODEL), lambda i, j: (layer, _proj_group(j), 0, 0)),
            pl.BlockSpec((None, 1, D_MODEL), lambda i, j: (jnp.clip(j - 4, 0, 1), 0, 0)),
        ],
        out_specs=[
            pl.BlockSpec((tm, D_MODEL), lambda i, j: (i, 0)),
            pl.BlockSpec((None, N_HEADS, tm, HEAD_DIM), lambda i, j: (jnp.clip(j - 1, 0, 2), 0, i, 0)),
            pl.BlockSpec((None, N_HEADS, tm, HEAD_DIM), lambda i, j: (jnp.clip(j - 4, 0, 1), 0, i, 0)),
            pl.BlockSpec((None, tm, D_MODEL), lambda i, j: (jnp.clip(j - 6, 0, 1), i, 0)),
        ],
        out_shape=[
            jax.ShapeDtypeStruct((n, D_MODEL), BF16),
            jax.ShapeDtypeStruct((3, N_HEADS, n, HEAD_DIM), BF16),
            jax.ShapeDtypeStruct((2, N_HEADS, n, HEAD_DIM), F32),
            jax.ShapeDtypeStruct((2, n, D_MODEL), BF16),
        ],
        scratch_shapes=[pltpu.VMEM((tm, D_MODEL), BF16)],
        compiler_params=_params(("parallel", "arbitrary"), VMEM_LIMIT), name="proj",
    )(h, w, w, lb)


def _conv_kernel(first_ref, last_ref, up_ref, u_ref, un_ref, w_ref, b_ref, g_ref, beta_ref, o_ref,
                 buf_ref, acc_ref, *, rows):
    i = pl.program_id(0)
    t = u_ref.shape[0]
    nslab = D_MODEL // 128
    prev = jnp.where(first_ref[i] == 1, 0.0, up_ref[...].astype(F32))
    nxt = jnp.where(last_ref[i] == 1, 0.0, un_ref[...].astype(F32))
    for c in range(nslab):
        cols = slice(c * 128, (c + 1) * 128)
        buf_ref[c, 0:HALO, :] = prev[:, cols]
        buf_ref[c, HALO:HALO + t, :] = u_ref[:, cols].astype(F32)
        buf_ref[c, HALO + t:2 * HALO + t, :] = nxt[:, cols]

    def slab(c, carry):
        for rb in range(t // rows):
            acc = jnp.zeros((rows, 128), F32)
            for k in range(CONV_WIDTH):
                r0 = rb * rows + HALO - CONV_PAD + k
                acc = acc + w_ref[c, k:k + 1, :] * buf_ref[c, r0:r0 + rows, :]
            acc_ref[c, rb * rows:(rb + 1) * rows, :] = acc
        return carry

    lax.fori_loop(0, nslab, slab, 0)
    x = jnp.concatenate([acc_ref[c] for c in range(nslab)], axis=-1) + b_ref[...]
    y = _layer_norm_rows(x, g_ref[...], beta_ref[...])
    o_ref[...] = (y * _sigmoid(y)).astype(BF16)


def _conv_call(u, first, last, w, b, g, beta, t):
    n = u.shape[0]
    nb, hb = n // t, t // HALO
    nslab = D_MODEL // 128
    wpad = jnp.zeros((32, D_MODEL), F32).at[:CONV_WIDTH].set(w).reshape(32, nslab, 128).transpose(1, 0, 2)
    vec = pl.BlockSpec((1, D_MODEL), lambda i, *_: (0, 0))
    return pl.pallas_call(
        functools.partial(_conv_kernel, rows=32),
        grid_spec=pltpu.PrefetchScalarGridSpec(
            num_scalar_prefetch=2, grid=(nb,),
            in_specs=[
                pl.BlockSpec((HALO, D_MODEL), lambda i, *_: (jnp.maximum(i * hb - 1, 0), 0)),
                pl.BlockSpec((t, D_MODEL), lambda i, *_: (i, 0)),
                pl.BlockSpec((HALO, D_MODEL), lambda i, *_: (jnp.minimum((i + 1) * hb, nb * hb - 1), 0)),
                pl.BlockSpec((nslab, 32, 128), lambda i, *_: (0, 0, 0)),
                vec, vec, vec,
            ],
            out_specs=pl.BlockSpec((t, D_MODEL), lambda i, *_: (i, 0)),
            scratch_shapes=[pltpu.VMEM((nslab, t + 2 * HALO, 128), F32), pltpu.VMEM((nslab, t, 128), F32)],
        ),
        out_shape=jax.ShapeDtypeStruct((n, D_MODEL), BF16),
        compiler_params=_params(("parallel",), VMEM_LIMIT), name="conv",
    )(first, last, u, u, u, wpad, b.reshape(1, -1), g.reshape(1, -1), beta.reshape(1, -1))


def _mul(a, b):
    if a is None:
        return b
    if b is None:
        return a
    return a * b


def _cat(tiles):
    return jnp.concatenate(tiles, axis=0)


def _nt(a, b):
    return lax.dot_general(a, b, (((1,), (1,)), ((), ())), preferred_element_type=F32)


def _tn(a, b):
    return lax.dot_general(a, b, (((0,), (0,)), ((), ())), preferred_element_type=F32)


def _gla_masks():
    i = lax.broadcasted_iota(jnp.int32, (CHUNK, CHUNK), 0)
    j = lax.broadcasted_iota(jnp.int32, (CHUNK, CHUNK), 1)
    rt, rs = i & 7, j & 7
    return [(rt >> 1) == (rs >> 1), (rt >> 2) == (rs >> 2), None]


def _block_products(tot, rev):
    ri = lax.broadcasted_iota(jnp.int32, tot.shape, 0)
    r = (7 - ri) if rev else ri

    def sh(x, d):
        return pltpu.roll(x, ((-d) if rev else d) % 8, 0)

    one = jnp.ones_like(tot)
    zero = jnp.zeros_like(tot)
    b0, b1, b2 = (r & 1) == 1, (r & 2) == 2, (r & 4) == 4
    m4 = r & 3
    d1, u1 = sh(tot, 1), sh(tot, -1)

    def excl_prefix(width):
        m = r & (width - 1)
        e = jnp.where(m >= 1, d1, one)
        step = 1
        while step < width:
            e = e * jnp.where(m >= step, sh(e, step), one)
            step *= 2
        return e

    def excl_suffix(width):
        m = r & (width - 1)
        e = jnp.where(m <= width - 2, u1, one)
        step = 1
        while step < width:
            e = e * jnp.where(m <= width - 1 - step, sh(e, -step), one)
            step *= 2
        return e

    lvl3 = (jnp.where(b0, one, zero), jnp.where(b0, zero, one))
    lvl4 = (jnp.where(b1, jnp.where(b0, d1, one), zero), jnp.where(b1, zero, jnp.where(b0, one, u1)))
    lvl5 = (jnp.where(b2, excl_prefix(4), zero), jnp.where(b2, zero, excl_suffix(4)))
    state = (excl_prefix(8), excl_suffix(8))
    del m4
    return [lvl3, lvl4, lvl5, state]


def _gla_prep(fs, qs, vs, rev):
    n = 8
    ks = [1.0 - f for f in fs]
    fi = [list(fs)]
    fi.append([fs[a] * fs[a - 1] if a & 1 else fs[a] for a in range(n)])
    fi.append([fi[1][a] * fi[1][(a & ~3) + 1] if a & 2 else fi[1][a] for a in range(n)])
    fi.append([fi[2][a] * fi[2][3] if a & 4 else fi[2][a] for a in range(n)])
    ge = [[None] * n]
    ge.append([fs[a + 1] if not a & 1 else None for a in range(n)])
    ge.append([_mul(ge[1][a], fi[1][(a & ~3) + 3]) if not a & 2 else ge[1][a] for a in range(n)])
    ge.append([_mul(ge[2][a], fi[2][7]) if not a & 4 else ge[2][a] for a in range(n)])

    low = [jnp.sum(qs[a] * ks[a], axis=-1, keepdims=True) * vs[a] for a in range(n)]
    for lvl in range(3):
        half = 1 << lvl
        kt = [None if (a >> lvl) & 1 else _mul(ks[a], ge[lvl][a]) for a in range(n)]
        for a in range(n):
            if not (a >> lvl) & 1:
                continue
            qt = qs[a] * fi[lvl][a]
            first = a & ~(2 * half - 1)
            for s in range(first, first + half):
                low[a] = low[a] + jnp.sum(qt * kt[s], axis=-1, keepdims=True) * vs[s]
    low = _cat(low)

    levels = []
    qf = [qs[a] * fi[3][a] for a in range(n)]
    kg = [_mul(ks[a], ge[3][a]) for a in range(n)]
    tot = fi[3][7]
    facs = _block_products(tot, rev)
    for lvl in range(3):
        rq, rk = facs[lvl]
        levels.append((_cat([x * rq for x in qf]).astype(BF16), _cat([x * rk for x in kg]).astype(BF16)))
    rq, rk = facs[3]
    q_in = _cat([x * rq for x in qf]).astype(BF16)
    k_out = _cat([x * rk for x in kg]).astype(BF16)
    last = 0 if rev else 7
    decay = (rq * tot)[last:last + 1, :]
    return levels, (_cat(vs).astype(BF16), q_in, k_out, decay, low)


def _gla_kernel(flag_ref, q_ref, v_ref, f_ref, *rest, rev, final):
    if final:
        obw_ref, og_ref, g_ref, o_ref, st_ref, qs_ref, vs_ref, os_ref = rest
    else:
        o_ref, st_ref, qs_ref, vs_ref = rest
        os_ref = o_ref
    nb = pl.num_programs(1)
    tb = pl.program_id(1)
    blk = (nb - 1 - tb) if rev else tb
    t = q_ref.shape[0]
    nchunks = t // CHUNK

    @pl.when(flag_ref[blk] == 1)
    def _():
        st_ref[...] = jnp.zeros_like(st_ref)

    qs_ref[...] = q_ref[...].astype(F32)
    vs_ref[...] = v_ref[...].astype(F32)
    masks = _gla_masks()
    order = list(range(7, -1, -1)) if rev else list(range(8))

    def rows_of(ci):
        c0 = ((nchunks - 1 - ci) if rev else ci) * CHUNK
        if not isinstance(c0, int):
            c0 = pl.multiple_of(c0, CHUNK)
        return [pl.ds(c0 + a, 8, stride=8) for a in order]

    def prep(ci):
        rows = rows_of(ci)
        return _gla_prep([f_ref[rw, :] for rw in rows], [qs_ref[rw, :] for rw in rows],
                         [vs_ref[rw, :] for rw in rows], rev)

    grp = min(GLA_GROUP, nchunks)
    ngroups = nchunks // grp

    def operands(g):
        return [prep(g * grp + u) for u in range(grp)]

    def products(ops):
        prods = []
        for levels, (vb, q_in, k_out, decay, low) in ops:
            att = None
            for (qt, kt), mask in zip(levels, masks):
                part = _nt(qt, kt)
                if mask is not None:
                    part = jnp.where(mask, part, 0.0)
                att = part if att is None else att + part
            prods.append((att.astype(BF16), vb, q_in, _tn(vb, k_out), decay, low))
        return prods

    def finish(g, prods, st):
        for u, (att, vb, q_in, kv, decay, low) in enumerate(prods):
            out = jnp.dot(att, vb, preferred_element_type=F32) + _nt(q_in, st.astype(BF16)) + low
            st = st * decay + kv
            for a, rw in enumerate(rows_of(g * grp + u)):
                os_ref[rw, :] = out[8 * a:8 * a + 8]
        return st

    st = st_ref[...]
    ops = [operands(g) for g in range(min(2, ngroups))]
    prods = products(ops.pop(0))
    for g in range(ngroups):
        st = finish(g, prods, st)
        if ops:
            prods = products(ops.pop(0))
        if g + 2 < ngroups:
            ops.append(operands(g + 2))
    st_ref[...] = st

    if final:
        o = os_ref[...] + obw_ref[...]
        o = o * lax.rsqrt(jnp.mean(o * o, axis=-1, keepdims=True) + RMS_EPS) * g_ref[...]
        o_ref[...] = (o * og_ref[...].astype(F32)).astype(BF16)


def _gla_call(flags, qvo, f, direction, t, obw=None, g=None):
    n = qvo.shape[2]
    nb = n // t
    rev = direction == 1
    final = not rev

    def tmap(hd, tb, *_):
        return (nb - 1 - tb) if rev else tb

    def sel(k):
        return pl.BlockSpec((None, None, t, HEAD_DIM), lambda hd, tb, *_: (k, hd, tmap(hd, tb), 0))

    in_specs = [sel(0), sel(1), sel(direction)]
    args = [qvo, qvo, f]
    scratch = [pltpu.VMEM((HEAD_DIM, HEAD_DIM), F32), pltpu.VMEM((t, HEAD_DIM), F32),
               pltpu.VMEM((t, HEAD_DIM), F32)]
    if final:
        in_specs += [pl.BlockSpec((None, t, HEAD_DIM), lambda hd, tb, *_: (hd, tb, 0)), sel(2),
                     pl.BlockSpec((None, 1, HEAD_DIM), lambda hd, tb, *_: (hd, 0, 0))]
        args += [obw, qvo, g.reshape(N_HEADS, 1, HEAD_DIM)]
        scratch.append(pltpu.VMEM((t, HEAD_DIM), F32))
    return pl.pallas_call(
        functools.partial(_gla_kernel, rev=rev, final=final),
        grid_spec=pltpu.PrefetchScalarGridSpec(
            num_scalar_prefetch=1, grid=(N_HEADS, nb), in_specs=in_specs,
            out_specs=pl.BlockSpec((None, t, HEAD_DIM), lambda hd, tb, *_: (hd, tmap(hd, tb), 0)),
            scratch_shapes=scratch,
        ),
        out_shape=jax.ShapeDtypeStruct((N_HEADS, n, HEAD_DIM), BF16 if final else F32),
        compiler_params=_params(("parallel", "arbitrary"), VMEM_LIMIT),
        name="gla_fwd" if final else "gla_bwd",
    )(flags, *args)


def _merge_kernel(h_ref, ua_ref, ob_ref, gab_ref, wc_ref, whg_ref, wo_ref, g_ref, b_ref, o_ref):
    ob = jnp.concatenate([ob_ref[hd] for hd in range(N_HEADS)], axis=-1)
    branch_a = jnp.dot(ua_ref[...], wc_ref[...], preferred_element_type=F32)
    branch_b = jnp.dot(ob, whg_ref[...], preferred_element_type=F32)
    merged = gab_ref[0].astype(F32) * branch_a + gab_ref[1].astype(F32) * branch_b
    mix = jnp.dot(merged.astype(BF16), wo_ref[...], preferred_element_type=F32)
    o_ref[...] = _layer_norm_rows(ALPHA * h_ref[...] + mix, g_ref[...], b_ref[...])


def _merge_call(h, ua, ob, gab, wc, whg, wo, g, b, layer, tm):
    n = h.shape[0]
    row = pl.BlockSpec((tm, D_MODEL), lambda i: (i, 0))
    mat = pl.BlockSpec((None, D_MODEL, D_MODEL), lambda i: (layer, 0, 0))
    vec = pl.BlockSpec((1, D_MODEL), lambda i: (0, 0))
    return pl.pallas_call(
        _merge_kernel, grid=(n // tm,),
        in_specs=[row, row, pl.BlockSpec((N_HEADS, tm, HEAD_DIM), lambda i: (0, i, 0)),
                  pl.BlockSpec((2, tm, D_MODEL), lambda i: (0, i, 0)), mat, mat, mat, vec, vec],
        out_specs=row, out_shape=jax.ShapeDtypeStruct((n, D_MODEL), F32),
        compiler_params=_params(("parallel",), VMEM_LIMIT), name="merge",
    )(h, ua, ob, gab, wc, whg, wo, g.reshape(1, -1), b.reshape(1, -1))


GATE_LANES = 128
CLASS_ROWS = 32


def _router_kernel(h_ref, w_ref, b_ref, cls_ref, rank_ref, gate_ref, count_ref):
    @pl.when(pl.program_id(0) == 0)
    def _():
        count_ref[...] = jnp.zeros_like(count_ref)

    logits = lax.dot_general(w_ref[...], h_ref[...].astype(BF16), (((1,), (1,)), ((), ())),
                             preferred_element_type=F32) + b_ref[...]
    rows = [logits[e:e + 1, :] for e in range(N_EXPERTS)]
    mx = functools.reduce(jnp.maximum, rows)
    ex = [jnp.exp(x - mx) for x in rows]
    den = functools.reduce(lambda a, b: a + b, ex)
    sc = [e / den for e in ex]
    gscore = []
    for grp in range(N_GROUPS):
        s = sc[grp * 4:grp * 4 + 4]
        gscore.append(functools.reduce(jnp.maximum, [s[a] + s[b] for a, b in PAIRS]))
    gsel = jnp.zeros_like(gscore[0], dtype=jnp.int32)
    best = gscore[0]
    for grp in range(1, N_GROUPS):
        better = gscore[grp] > best
        gsel = jnp.where(better, grp, gsel)
        best = jnp.where(better, gscore[grp], best)
    ing = []
    for k in range(EXPERTS_PER_GROUP):
        x = sc[k]
        for grp in range(1, N_GROUPS):
            x = jnp.where(gsel == grp, sc[grp * 4 + k], x)
        ing.append(x)
    i1 = jnp.zeros_like(gsel)
    w1 = ing[0]
    for k in range(1, 4):
        better = ing[k] > w1
        i1 = jnp.where(better, k, i1)
        w1 = jnp.where(better, ing[k], w1)
    i2 = jnp.full_like(gsel, -1)
    w2 = jnp.full_like(w1, -1.0)
    for k in range(4):
        better = (i1 != k) & (ing[k] > w2)
        i2 = jnp.where(better, k, i2)
        w2 = jnp.where(better, ing[k], w2)
    tot = w1 + w2
    g1, g2 = w1 / tot, w2 / tot
    lo, hi = jnp.minimum(i1, i2), jnp.maximum(i1, i2)
    pair = jnp.where(lo == 0, hi - 1, jnp.where(lo == 1, hi + 1, 5))
    cls = gsel * len(PAIRS) + pair
    cls_ref[...] = cls
    tm = cls.shape[1]

    onehot = lax.broadcasted_iota(jnp.int32, (CLASS_ROWS, tm), 0) == cls
    upper = (lax.broadcasted_iota(jnp.int32, (tm, tm), 0) <= lax.broadcasted_iota(jnp.int32, (tm, tm), 1))
    prefix = jnp.dot(jnp.where(onehot, 1.0, 0.0).astype(BF16), jnp.where(upper, 1.0, 0.0).astype(BF16),
                     preferred_element_type=F32)
    seen = count_ref[:, 0:1]
    rank_ref[...] = jnp.sum(jnp.where(onehot, prefix + seen - 1.0, 0.0), axis=0, keepdims=True).astype(jnp.int32)
    count_ref[...] = count_ref[...] + prefix[:, tm - 1:tm]

    first_is_lo = i1 < i2
    gates = (jnp.where(first_is_lo, g1, g2), jnp.where(first_is_lo, g2, g1))
    rid = lax.broadcasted_iota(jnp.int32, (16, tm), 0)
    pieces = jnp.zeros((16, tm), F32)
    rest = list(gates)
    for part in range(3):
        for k in range(2):
            piece = rest[k].astype(BF16).astype(F32)
            rest[k] = rest[k] - piece
            pieces = jnp.where(rid == 2 * part + k, piece, pieces)
    lane = lax.broadcasted_iota(jnp.int32, (16, GATE_LANES), 1)
    row = lax.broadcasted_iota(jnp.int32, (16, GATE_LANES), 0)
    place = jnp.where((row < 6) & (lane == (row & 1)), 1.0, 0.0).astype(BF16)
    gate_ref[...] = _tn(pieces.astype(BF16), place)


def _router_call(h, w_t, b, tm):
    n = h.shape[0]
    return pl.pallas_call(
        _router_kernel, grid=(n // tm,),
        in_specs=[pl.BlockSpec((tm, D_MODEL), lambda i: (i, 0)),
                  pl.BlockSpec((N_EXPERTS, D_MODEL), lambda i: (0, 0)),
                  pl.BlockSpec((N_EXPERTS, 1), lambda i: (0, 0))],
        out_specs=[pl.BlockSpec((1, tm), lambda i: (0, i)), pl.BlockSpec((1, tm), lambda i: (0, i)),
                   pl.BlockSpec((tm, GATE_LANES), lambda i: (i, 0)),
                   pl.BlockSpec((CLASS_ROWS, 128), lambda i: (0, 0))],
        out_shape=[jax.ShapeDtypeStruct((1, n), jnp.int32), jax.ShapeDtypeStruct((1, n), jnp.int32),
                   jax.ShapeDtypeStruct((n, GATE_LANES), F32), jax.ShapeDtypeStruct((CLASS_ROWS, 128), F32)],
        compiler_params=_params(("arbitrary",)), name="router",
    )(h, w_t, b)


ROW_WIDTH = D_MODEL + GATE_LANES


def _scatter_kernel(dest_ref, h_ref, g_ref, init_ref, o_ref, row_ref, sem):
    del init_ref
    tm = h_ref.shape[0]
    base = pl.program_id(0) * tm
    row_ref[:, 0:D_MODEL] = h_ref[...]
    row_ref[:, D_MODEL:ROW_WIDTH] = g_ref[...]

    for r in range(tm):
        pltpu.make_async_copy(row_ref.at[pl.ds(r, 1), :], o_ref.at[pl.ds(dest_ref[base + r], 1), :], sem).start()
    pltpu.make_async_copy(row_ref, o_ref.at[pl.ds(0, tm), :], sem).wait()


def _scatter_call(h, gates, dest, n_out, tm):
    n = h.shape[0]
    return pl.pallas_call(
        _scatter_kernel,
        grid_spec=pltpu.PrefetchScalarGridSpec(
            num_scalar_prefetch=1, grid=(n // tm,),
            in_specs=[pl.BlockSpec((tm, D_MODEL), lambda i, *_: (i, 0)),
                      pl.BlockSpec((tm, GATE_LANES), lambda i, *_: (i, 0)),
                      pl.BlockSpec(memory_space=pl.ANY)],
            out_specs=pl.BlockSpec(memory_space=pl.ANY),
            scratch_shapes=[pltpu.VMEM((tm, ROW_WIDTH), F32), pltpu.SemaphoreType.DMA(())],
        ),
        out_shape=jax.ShapeDtypeStruct((n_out, ROW_WIDTH), F32),
        input_output_aliases={3: 0},
        compiler_params=_params(("arbitrary",)), name="scatter",
    )(dest, h, gates, jnp.zeros((n_out, ROW_WIDTH), F32))


def _gather_ln_kernel(dest_ref, h_ref, y_ref, g_ref, b_ref, *rest, split):
    *o_refs, row_ref, sem = rest
    tm = h_ref.shape[0]
    i = pl.program_id(0)
    base = i * tm

    for r in range(tm):
        pltpu.make_async_copy(y_ref.at[pl.ds(dest_ref[base + r], 1), :], row_ref.at[pl.ds(r, 1), :], sem).start()
    pltpu.make_async_copy(y_ref.at[pl.ds(0, tm), :], row_ref, sem).wait()
    out = _layer_norm_rows(ALPHA * h_ref[...] + row_ref[...], g_ref[...], b_ref[...])
    if split is None:
        o_refs[0][...] = out
    else:
        @pl.when(i < split)
        def _():
            o_refs[0][...] = out

        @pl.when(i >= split)
        def _():
            o_refs[1][...] = out


def _gather_ln_call(h, ys, dest, g, b, tm, split_rows=None):
    n = h.shape[0]
    row = pl.BlockSpec((tm, D_MODEL), lambda i, *_: (i, 0))
    vec = pl.BlockSpec((1, D_MODEL), lambda i, *_: (0, 0))
    if split_rows is None:
        split, out_specs, out_shape = None, row, jax.ShapeDtypeStruct((n, D_MODEL), F32)
    else:
        split = split_rows // tm
        out_specs = [pl.BlockSpec((tm, D_MODEL), lambda i, *_: (jnp.minimum(i, split - 1), 0)),
                     pl.BlockSpec((tm, D_MODEL), lambda i, *_: (jnp.maximum(i - split, 0), 0))]
        out_shape = [jax.ShapeDtypeStruct((split_rows, D_MODEL), F32),
                     jax.ShapeDtypeStruct((n - split_rows, D_MODEL), F32)]
    return pl.pallas_call(
        functools.partial(_gather_ln_kernel, split=split),
        grid_spec=pltpu.PrefetchScalarGridSpec(
            num_scalar_prefetch=1, grid=(n // tm,),
            in_specs=[row, pl.BlockSpec(memory_space=pl.ANY), vec, vec],
            out_specs=out_specs,
            scratch_shapes=[pltpu.VMEM((tm, D_MODEL), F32), pltpu.SemaphoreType.DMA(())],
        ),
        out_shape=out_shape,
        compiler_params=_params(("arbitrary",)), name="gather_ln",
    )(dest, h, ys, g.reshape(1, -1), b.reshape(1, -1))


def _gelu_tanh(x):
    return 0.5 * x * (1.0 + jnp.tanh(0.7978845608028654 * (x + 0.044715 * x * x * x)))


def _moe_kernel(ea_ref, eb_ref, used_ref, x_ref, w1a_ref, w2a_ref, w1b_ref, w2b_ref, o_ref):
    @pl.when(pl.program_id(0) < used_ref[0])
    def _():
        x = x_ref[:, 0:D_MODEL].astype(BF16)

        def expert(w1_ref, w2_ref):
            hid = _gelu_tanh(jnp.dot(x, w1_ref[...], preferred_element_type=F32))
            return jnp.dot(hid.astype(BF16), w2_ref[...], preferred_element_type=F32)

        g = x_ref[:, D_MODEL:ROW_WIDTH]
        o_ref[...] = g[:, 0:1] * expert(w1a_ref, w2a_ref) + g[:, 1:2] * expert(w1b_ref, w2b_ref)

    @pl.when(pl.program_id(0) >= used_ref[0])
    def _():
        o_ref[...] = jnp.zeros_like(o_ref)


def _moe_call(xs, blk_ea, blk_eb, n_used, w1, w2, layer, tm):
    npad = xs.shape[0]
    w1a = pl.BlockSpec((None, None, D_MODEL, D_EXPERT), lambda i, ea, eb, u: (layer, ea[i], 0, 0))
    w2a = pl.BlockSpec((None, None, D_EXPERT, D_MODEL), lambda i, ea, eb, u: (layer, ea[i], 0, 0))
    w1b = pl.BlockSpec((None, None, D_MODEL, D_EXPERT), lambda i, ea, eb, u: (layer, eb[i], 0, 0))
    w2b = pl.BlockSpec((None, None, D_EXPERT, D_MODEL), lambda i, ea, eb, u: (layer, eb[i], 0, 0))
    return pl.pallas_call(
        _moe_kernel,
        grid_spec=pltpu.PrefetchScalarGridSpec(
            num_scalar_prefetch=3, grid=(npad // tm,),
            in_specs=[pl.BlockSpec((tm, ROW_WIDTH), lambda i, *_: (i, 0)), w1a, w2a, w1b, w2b],
            out_specs=pl.BlockSpec((tm, D_MODEL), lambda i, *_: (i, 0)),
        ),
        out_shape=jax.ShapeDtypeStruct((npad, D_MODEL), F32),
        compiler_params=_params(("arbitrary",), VMEM_LIMIT), name="moe",
    )(blk_ea, blk_eb, n_used, xs, w1, w2, w1, w2)


def _routing_tables(cls, rank, counts, tm):
    nblk = cls.shape[0] // tm + N_CLASSES
    padded = (counts + tm - 1) // tm * tm
    pad_end = jnp.cumsum(padded)
    classes = jnp.arange(N_CLASSES, dtype=jnp.int32)
    dest = rank + jnp.sum(jnp.where(cls[:, None] == classes, pad_end - padded, 0), axis=1)
    n_used = pad_end[-1] // tm
    first_row = jnp.minimum(jnp.arange(nblk, dtype=jnp.int32), n_used - 1) * tm
    blk_cls = jnp.minimum(jnp.sum(first_row[:, None] >= pad_end, axis=1).astype(jnp.int32), N_CLASSES - 1)
    grp, pr = blk_cls // len(PAIRS), blk_cls % len(PAIRS)
    lo = (pr >= 3).astype(jnp.int32) + (pr >= 5).astype(jnp.int32)
    hi = jnp.where(pr < 3, pr + 1, jnp.where(pr < 5, pr - 1, 3))
    blk_ea = grp * EXPERTS_PER_GROUP + lo
    blk_eb = grp * EXPERTS_PER_GROUP + hi
    return dest, nblk * tm, blk_ea, blk_eb, n_used.reshape(1).astype(jnp.int32)


def _block_flags(seq_lens, t):
    first, last = [], []
    for length in seq_lens:
        nb = length // t
        first += [1] + [0] * (nb - 1)
        last += [0] * (nb - 1) + [1]
    return jnp.asarray(np.array(first, np.int32)), jnp.asarray(np.array(last, np.int32))


def _trunk(xa, xb, seq_lens, p, *, tm, tm_proj, t_conv, t_gla, tm_moe):
    (ln_in_g, ln_in_b, w_in, dw_w, dw_b, conv_ln_g, conv_ln_b, w_conv_out, hg_lower, hg_norm_g, w_hg_out,
     w_o, ln1_g, ln1_b, w_router, b_router, w_e1, w_e2, ln2_g, ln2_b) = p
    depth = w_in.shape[0]
    conv_first, conv_last = _block_flags(seq_lens, t_conv)
    gla_first, gla_last = _block_flags(seq_lens, t_gla)

    lb_p = jax.nn.softmax(hg_lower.astype(F32), axis=0)
    lb_all = jnp.clip(jnp.cumsum(lb_p, axis=0) - lb_p[0:1], 0.0, 1.0)
    w_in_b = w_in.reshape(depth, D_MODEL, 9, D_MODEL).transpose(0, 2, 1, 3).astype(BF16)
    w_router_t = w_router.T.astype(BF16)
    b_router_c = b_router.reshape(N_EXPERTS, 1).astype(F32)
    wc_b, whg_b, wo_b = w_conv_out.astype(BF16), w_hg_out.astype(BF16), w_o.astype(BF16)
    w_e1_b, w_e2_b = w_e1.astype(BF16), w_e2.astype(BF16)

    h = _ln_call(xa, xb, ln_in_g, ln_in_b, tm)
    for l in range(depth):
        u, qvo, f, gab = _proj_call(h, w_in_b, lb_all[l].reshape(2, 1, D_MODEL), l, tm_proj)
        ua = _conv_call(u, conv_first, conv_last, dw_w[l], dw_b[l], conv_ln_g[l], conv_ln_b[l], t_conv)
        o_bw = _gla_call(gla_last, qvo, f, 1, t_gla)
        ob = _gla_call(gla_first, qvo, f, 0, t_gla, obw=o_bw, g=hg_norm_g[l].astype(F32))
        h1 = _merge_call(h, ua, ob, gab, wc_b, whg_b, wo_b, ln1_g[l], ln1_b[l], l, tm)
        cls, rank, gates, counts = _router_call(h1, w_router_t, b_router_c, tm)
        dest, n_rows, blk_ea, blk_eb, n_used = _routing_tables(
            cls[0], rank[0], counts[:N_CLASSES, 0].astype(jnp.int32), tm_moe)
        xs = _scatter_call(h1, gates, dest, n_rows, tm)
        ys = _moe_call(xs, blk_ea, blk_eb, n_used, w_e1_b, w_e2_b, l, tm_moe)
        h = _gather_ln_call(h1, ys, dest, ln2_g[l], ln2_b[l], tm,
                            split_rows=xa.shape[0] if l == depth - 1 else None)
    return h


def kernel(x_prompt, x_sample, ln_in_g, ln_in_b, w_in, dw_w, dw_b, conv_ln_g, conv_ln_b, w_conv_out, hg_lower,
           hg_norm_g, w_hg_out, w_o, ln1_g, ln1_b, w_router, b_router, w_e1, w_e2, ln2_g, ln2_b):
    bp, lp, d = x_prompt.shape
    bs, ls, _ = x_sample.shape
    seq_lens = (lp,) * bp + (ls,) * bs
    p = (ln_in_g, ln_in_b, w_in, dw_w, dw_b, conv_ln_g, conv_ln_b, w_conv_out, hg_lower, hg_norm_g, w_hg_out,
         w_o, ln1_g, ln1_b, w_router, b_router, w_e1, w_e2, ln2_g, ln2_b)
    yp, ys = _trunk(x_prompt.reshape(bp * lp, d), x_sample.reshape(bs * ls, d), seq_lens, p,
                    tm=512, tm_proj=1024, t_conv=512, t_gla=2048, tm_moe=256)
    return (yp.reshape(bp, lp, d), ys.reshape(bs, ls, d))
```

```python
import functools
import math

import numpy as np
import jax
import jax.numpy as jnp
from jax import lax
from jax.experimental import pallas as pl
from jax.experimental.pallas import tpu as pltpu

F32 = jnp.float32
BF16 = jnp.bfloat16

D_MODEL = 1024
DEPTH = 4
HEAD_DIM = 128
N_HEADS = D_MODEL // HEAD_DIM
CONV_WIDTH = 31
CONV_PAD = CONV_WIDTH // 2
HALO = 16
CHUNK = 64
GLA_GROUP = 4
EXP_CLAMP = 60.0
N_EXPERTS = 16
N_GROUPS = 4
EXPERTS_PER_GROUP = 4
D_EXPERT = 2 * D_MODEL
PAIRS = ((0, 1), (0, 2), (0, 3), (1, 2), (1, 3), (2, 3))
N_CLASSES = N_GROUPS * len(PAIRS)
ALPHA = (2 * DEPTH) ** 0.25
LN_EPS = 1e-5
RMS_EPS = 1e-6
VMEM_LIMIT = 56 * 1024 * 1024


def _params(sem, vmem=None):
    return pltpu.CompilerParams(dimension_semantics=sem, vmem_limit_bytes=vmem)


def _sigmoid(x):
    return 1.0 / (1.0 + jnp.exp(-x))


def _layer_norm_rows(x, g, b):
    mu = jnp.mean(x, axis=-1, keepdims=True)
    xc = x - mu
    var = jnp.mean(xc * xc, axis=-1, keepdims=True)
    return xc * lax.rsqrt(var + LN_EPS) * g + b


def _ln_kernel(xa_ref, xb_ref, g_ref, b_ref, o_ref, *, split):
    i = pl.program_id(0)

    @pl.when(i < split)
    def _():
        o_ref[...] = _layer_norm_rows(xa_ref[...], g_ref[...], b_ref[...])

    @pl.when(i >= split)
    def _():
        o_ref[...] = _layer_norm_rows(xb_ref[...], g_ref[...], b_ref[...])


def _ln_call(xa, xb, g, b, tm):
    na, nb = xa.shape[0], xb.shape[0]
    split = na // tm
    vec = pl.BlockSpec((1, D_MODEL), lambda i: (0, 0))
    return pl.pallas_call(
        functools.partial(_ln_kernel, split=split), grid=((na + nb) // tm,),
        in_specs=[pl.BlockSpec((tm, D_MODEL), lambda i: (jnp.minimum(i, split - 1), 0)),
                  pl.BlockSpec((tm, D_MODEL), lambda i: (jnp.maximum(i - split, 0), 0)), vec, vec],
        out_specs=pl.BlockSpec((tm, D_MODEL), lambda i: (i, 0)),
        out_shape=jax.ShapeDtypeStruct((na + nb, D_MODEL), F32),
        compiler_params=_params(("arbitrary",)), name="ln_in",
    )(xa, xb, g.reshape(1, -1), b.reshape(1, -1))


PROJ_COLS = 256


def _proj_kernel(x_ref, wv_ref, w_ref, lb_ref, u_ref, qvo_ref, f_ref, gab_ref, xb_ref):
    j = pl.program_id(1)

    @pl.when(j == 0)
    def _():
        xb_ref[...] = x_ref[...].astype(BF16)

    x = xb_ref[...]
    pieces = [slice(c * PROJ_COLS, (c + 1) * PROJ_COLS) for c in range(D_MODEL // PROJ_COLS)]

    def mm(ref, cols):
        return jnp.dot(x, ref[:, cols], preferred_element_type=F32)

    def heads_store(ref, cols, val):
        for k in range(PROJ_COLS // HEAD_DIM):
            ref[cols.start // HEAD_DIM + k] = val[:, k * HEAD_DIM:(k + 1) * HEAD_DIM].astype(ref.dtype)

    @pl.when(j == 0)
    def _():
        for cols in pieces:
            u_ref[:, cols] = (mm(wv_ref, cols) * _sigmoid(mm(w_ref, cols))).astype(BF16)

    @pl.when(j == 1)
    def _():
        for cols in pieces:
            a = mm(w_ref, cols)
            heads_store(qvo_ref, cols, a * _sigmoid(a) * HEAD_DIM ** -0.5)

    @pl.when(j == 2)
    def _():
        for cols in pieces:
            heads_store(qvo_ref, cols, mm(w_ref, cols))

    @pl.when(j == 3)
    def _():
        for cols in pieces:
            a = mm(w_ref, cols)
            heads_store(qvo_ref, cols, a * _sigmoid(a))

    @pl.when((j == 4) | (j == 5))
    def _():
        for cols in pieces:
            t = jnp.exp(-mm(w_ref, cols))
            f = (1.0 + lb_ref[:, cols] * jnp.minimum(t, math.exp(EXP_CLAMP))) / (1.0 + t)
            heads_store(f_ref, cols, jnp.minimum(f, 1.0))

    @pl.when(j >= 6)
    def _():
        for cols in pieces:
            gab_ref[:, cols] = _sigmoid(mm(w_ref, cols)).astype(BF16)


def _proj_group(j):
    t = j + 1
    return jnp.where((t == 3) | (t == 4), t + 2, jnp.where((t == 5) | (t == 6), t - 2, t))


def _proj_call(h, w, lb, layer, tm):
    n = h.shape[0]
    return pl.pallas_call(
        _proj_kernel, grid=(n // tm, 8),
        in_specs=[
            pl.BlockSpec((tm, D_MODEL), lambda i, j: (i, 0)),
            pl.BlockSpec((None, None, D_MODEL, D_MODEL), lambda i, j: (layer, 0, 0, 0)),
            pl.BlockSpec((None, None, D_MODEL, D_MODEL), lambda i, j: (layer, _proj_group(j), 0, 0)),
            pl.BlockSpec((None, 1, D_MODEL), lambda i, j: (jnp.clip(j - 4, 0, 1), 0, 0)),
        ],
        out_specs=[
            pl.BlockSpec((tm, D_MODEL), lambda i, j: (i, 0)),
            pl.BlockSpec((None, N_HEADS, tm, HEAD_DIM), lambda i, j: (jnp.clip(j - 1, 0, 2), 0, i, 0)),
            pl.BlockSpec((None, N_HEADS, tm, HEAD_DIM), lambda i, j: (jnp.clip(j - 4, 0, 1), 0, i, 0)),
            pl.BlockSpec((None, tm, D_MODEL), lambda i, j: (jnp.clip(j - 6, 0, 1), i, 0)),
        ],
        out_shape=[
            jax.ShapeDtypeStruct((n, D_MODEL), BF16),
            jax.ShapeDtypeStruct((3, N_HEADS, n, HEAD_DIM), BF16),
            jax.ShapeDtypeStruct((2, N_HEADS, n, HEAD_DIM), F32),
            jax.ShapeDtypeStruct((2, n, D_MODEL), BF16),
        ],
        scratch_shapes=[pltpu.VMEM((tm, D_MODEL), BF16)],
        compiler_params=_params(("parallel", "arbitrary"), VMEM_LIMIT), name="proj",
    )(h, w, w, lb)


def _conv_kernel(first_ref, last_ref, up_ref, u_ref, un_ref, w_ref, b_ref, g_ref, beta_ref, o_ref,
                 buf_ref, acc_ref, *, rows):
    i = pl.program_id(0)
    t = u_ref.shape[0]
    nslab = D_MODEL // 128
    prev = jnp.where(first_ref[i] == 1, 0.0, up_ref[...].astype(F32))
    nxt = jnp.where(last_ref[i] == 1, 0.0, un_ref[...].astype(F32))
    for c in range(nslab):
        cols = slice(c * 128, (c + 1) * 128)
        buf_ref[c, 0:HALO, :] = prev[:, cols]
        buf_ref[c, HALO:HALO + t, :] = u_ref[:, cols].astype(F32)
        buf_ref[c, HALO + t:2 * HALO + t, :] = nxt[:, cols]

    def slab(c, carry):
        for rb in range(t // rows):
            acc = jnp.zeros((rows, 128), F32)
            for k in range(CONV_WIDTH):
                r0 = rb * rows + HALO - CONV_PAD + k
                acc = acc + w_ref[c, k:k + 1, :] * buf_ref[c, r0:r0 + rows, :]
            acc_ref[c, rb * rows:(rb + 1) * rows, :] = acc
        return carry

    lax.fori_loop(0, nslab, slab, 0)
    x = jnp.concatenate([acc_ref[c] for c in range(nslab)], axis=-1) + b_ref[...]
    y = _layer_norm_rows(x, g_ref[...], beta_ref[...])
    o_ref[...] = (y * _sigmoid(y)).astype(BF16)


def _conv_call(u, first, last, w, b, g, beta, t):
    n = u.shape[0]
    nb, hb = n // t, t // HALO
    nslab = D_MODEL // 128
    wpad = jnp.zeros((32, D_MODEL), F32).at[:CONV_WIDTH].set(w).reshape(32, nslab, 128).transpose(1, 0, 2)
    vec = pl.BlockSpec((1, D_MODEL), lambda i, *_: (0, 0))
    return pl.pallas_call(
        functools.partial(_conv_kernel, rows=32),
        grid_spec=pltpu.PrefetchScalarGridSpec(
            num_scalar_prefetch=2, grid=(nb,),
            in_specs=[
                pl.BlockSpec((HALO, D_MODEL), lambda i, *_: (jnp.maximum(i * hb - 1, 0), 0)),
                pl.BlockSpec((t, D_MODEL), lambda i, *_: (i, 0)),
                pl.BlockSpec((HALO, D_MODEL), lambda i, *_: (jnp.minimum((i + 1) * hb, nb * hb - 1), 0)),
                pl.BlockSpec((nslab, 32, 128), lambda i, *_: (0, 0, 0)),
                vec, vec, vec,
            ],
            out_specs=pl.BlockSpec((t, D_MODEL), lambda i, *_: (i, 0)),
            scratch_shapes=[pltpu.VMEM((nslab, t + 2 * HALO, 128), F32), pltpu.VMEM((nslab, t, 128), F32)],
        ),
        out_shape=jax.ShapeDtypeStruct((n, D_MODEL), BF16),
        compiler_params=_params(("parallel",), VMEM_LIMIT), name="conv",
    )(first, last, u, u, u, wpad, b.reshape(1, -1), g.reshape(1, -1), beta.reshape(1, -1))


def _mul(a, b):
    if a is None:
        return b
    if b is None:
        return a
    return a * b


def _cat(tiles):
    return jnp.concatenate(tiles, axis=0)


def _nt(a, b):
    return lax.dot_general(a, b, (((1,), (1,)), ((), ())), preferred_element_type=F32)


def _tn(a, b):
    return lax.dot_general(a, b, (((0,), (0,)), ((), ())), preferred_element_type=F32)


def _gla_masks():
    i = lax.broadcasted_iota(jnp.int32, (CHUNK, CHUNK), 0)
    j = lax.broadcasted_iota(jnp.int32, (CHUNK, CHUNK), 1)
    rt, rs = i & 7, j & 7
    return [(rt >> 1) == (rs >> 1), (rt >> 2) == (rs >> 2), None]


def _block_products(tot, rev):
    ri = lax.broadcasted_iota(jnp.int32, tot.shape, 0)
    r = (7 - ri) if rev else ri

    def sh(x, d):
        return pltpu.roll(x, ((-d) if rev else d) % 8, 0)

    one = jnp.ones_like(tot)
    zero = jnp.zeros_like(tot)
    b0, b1, b2 = (r & 1) == 1, (r & 2) == 2, (r & 4) == 4
    m4 = r & 3
    d1, u1 = sh(tot, 1), sh(tot, -1)

    def excl_prefix(width):
        m = r & (width - 1)
        e = jnp.where(m >= 1, d1, one)
        step = 1
        while step < width:
            e = e * jnp.where(m >= step, sh(e, step), one)
            step *= 2
        return e

    def excl_suffix(width):
        m = r & (width - 1)
        e = jnp.where(m <= width - 2, u1, one)
        step = 1
        while step < width:
            e = e * jnp.where(m <= width - 1 - step, sh(e, -step), one)
            step *= 2
        return e

    lvl3 = (jnp.where(b0, one, zero), jnp.where(b0, zero, one))
    lvl4 = (jnp.where(b1, jnp.where(b0, d1, one), zero), jnp.where(b1, zero, jnp.where(b0, one, u1)))
    lvl5 = (jnp.where(b2, excl_prefix(4), zero), jnp.where(b2, zero, excl_suffix(4)))
    state = (excl_prefix(8), excl_suffix(8))
    del m4
    return [lvl3, lvl4, lvl5, state]


def _gla_prep(fs, qs, vs, rev):
    n = 8
    ks = [1.0 - f for f in fs]
    fi = [list(fs)]
    fi.append([fs[a] * fs[a - 1] if a & 1 else fs[a] for a in range(n)])
    fi.append([fi[1][a] * fi[1][(a & ~3) + 1] if a & 2 else fi[1][a] for a in range(n)])
    fi.append([fi[2][a] * fi[2][3] if a & 4 else fi[2][a] for a in range(n)])
    ge = [[None] * n]
    ge.append([fs[a + 1] if not a & 1 else None for a in range(n)])
    ge.append([_mul(ge[1][a], fi[1][(a & ~3) + 3]) if not a & 2 else ge[1][a] for a in range(n)])
    ge.append([_mul(ge[2][a], fi[2][7]) if not a & 4 else ge[2][a] for a in range(n)])

    low = [jnp.sum(qs[a] * ks[a], axis=-1, keepdims=True) * vs[a] for a in range(n)]
    for lvl in range(3):
        half = 1 << lvl
        kt = [None if (a >> lvl) & 1 else _mul(ks[a], ge[lvl][a]) for a in range(n)]
        for a in range(n):
            if not (a >> lvl) & 1:
                continue
            qt = qs[a] * fi[lvl][a]
            first = a & ~(2 * half - 1)
            for s in range(first, first + half):
                low[a] = low[a] + jnp.sum(qt * kt[s], axis=-1, keepdims=True) * vs[s]
    low = _cat(low)

    levels = []
    qf = [qs[a] * fi[3][a] for a in range(n)]
    kg = [_mul(ks[a], ge[3][a]) for a in range(n)]
    tot = fi[3][7]
    facs = _block_products(tot, rev)
    for lvl in range(3):
        rq, rk = facs[lvl]
        levels.append((_cat([x * rq for x in qf]).astype(BF16), _cat([x * rk for x in kg]).astype(BF16)))
    rq, rk = facs[3]
    q_in = _cat([x * rq for x in qf]).astype(BF16)
    k_out = _cat([x * rk for x in kg]).astype(BF16)
    last = 0 if rev else 7
    decay = (rq * tot)[last:last + 1, :]
    return levels, (_cat(vs).astype(BF16), q_in, k_out, decay, low)


def _gla_kernel(flag_ref, q_ref, v_ref, f_ref, *rest, rev, final):
    if final:
        obw_ref, og_ref, g_ref, o_ref, st_ref, qs_ref, vs_ref, os_ref = rest
    else:
        o_ref, st_ref, qs_ref, vs_ref = rest
        os_ref = o_ref
    nb = pl.num_programs(1)
    tb = pl.program_id(1)
    blk = (nb - 1 - tb) if rev else tb
    t = q_ref.shape[0]
    nchunks = t // CHUNK

    @pl.when(flag_ref[blk] == 1)
    def _():
        st_ref[...] = jnp.zeros_like(st_ref)

    qs_ref[...] = q_ref[...].astype(F32)
    vs_ref[...] = v_ref[...].astype(F32)
    masks = _gla_masks()
    order = list(range(7, -1, -1)) if rev else list(range(8))

    def rows_of(ci):
        c0 = ((nchunks - 1 - ci) if rev else ci) * CHUNK
        if not isinstance(c0, int):
            c0 = pl.multiple_of(c0, CHUNK)
        return [pl.ds(c0 + a, 8, stride=8) for a in order]

    def prep(ci):
        rows = rows_of(ci)
        return _gla_prep([f_ref[rw, :] for rw in rows], [qs_ref[rw, :] for rw in rows],
                         [vs_ref[rw, :] for rw in rows], rev)

    grp = min(GLA_GROUP, nchunks)
    ngroups = nchunks // grp

    def operands(g):
        return [prep(g * grp + u) for u in range(grp)]

    def products(ops):
        prods = []
        for levels, (vb, q_in, k_out, decay, low) in ops:
            att = None
            for (qt, kt), mask in zip(levels, masks):
                part = _nt(qt, kt)
                if mask is not None:
                    part = jnp.where(mask, part, 0.0)
                att = part if att is None else att + part
            prods.append((att.astype(BF16), vb, q_in, _tn(vb, k_out), decay, low))
        return prods

    def finish(g, prods, st):
        for u, (att, vb, q_in, kv, decay, low) in enumerate(prods):
            out = jnp.dot(att, vb, preferred_element_type=F32) + _nt(q_in, st.astype(BF16)) + low
            st = st * decay + kv
            for a, rw in enumerate(rows_of(g * grp + u)):
                os_ref[rw, :] = out[8 * a:8 * a + 8]
        return st

    st = st_ref[...]
    ops = [operands(g) for g in range(min(2, ngroups))]
    prods = products(ops.pop(0))
    for g in range(ngroups):
        st = finish(g, prods, st)
        if ops:
            prods = products(ops.pop(0))
        if g + 2 < ngroups:
            ops.append(operands(g + 2))
    st_ref[...] = st

    if final:
        o = os_ref[...] + obw_ref[...]
        o = o * lax.rsqrt(jnp.mean(o * o, axis=-1, keepdims=True) + RMS_EPS) * g_ref[...]
        o_ref[...] = (o * og_ref[...].astype(F32)).astype(BF16)


def _gla_call(flags, qvo, f, direction, t, obw=None, g=None):
    n = qvo.shape[2]
    nb = n // t
    rev = direction == 1
    final = not rev

    def tmap(hd, tb, *_):
        return (nb - 1 - tb) if rev else tb

    def sel(k):
        return pl.BlockSpec((None, None, t, HEAD_DIM), lambda hd, tb, *_: (k, hd, tmap(hd, tb), 0))

    in_specs = [sel(0), sel(1), sel(direction)]
    args = [qvo, qvo, f]
    scratch = [pltpu.VMEM((HEAD_DIM, HEAD_DIM), F32), pltpu.VMEM((t, HEAD_DIM), F32),
               pltpu.VMEM((t, HEAD_DIM), F32)]
    if final:
        in_specs += [pl.BlockSpec((None, t, HEAD_DIM), lambda hd, tb, *_: (hd, tb, 0)), sel(2),
                     pl.BlockSpec((None, 1, HEAD_DIM), lambda hd, tb, *_: (hd, 0, 0))]
        args += [obw, qvo, g.reshape(N_HEADS, 1, HEAD_DIM)]
        scratch.append(pltpu.VMEM((t, HEAD_DIM), F32))
    return pl.pallas_call(
        functools.partial(_gla_kernel, rev=rev, final=final),
        grid_spec=pltpu.PrefetchScalarGridSpec(
            num_scalar_prefetch=1, grid=(N_HEADS, nb), in_specs=in_specs,
            out_specs=pl.BlockSpec((None, t, HEAD_DIM), lambda hd, tb, *_: (hd, tmap(hd, tb), 0)),
            scratch_shapes=scratch,
        ),
        out_shape=jax.ShapeDtypeStruct((N_HEADS, n, HEAD_DIM), BF16 if final else F32),
        compiler_params=_params(("parallel", "arbitrary"), VMEM_LIMIT),
        name="gla_fwd" if final else "gla_bwd",
    )(flags, *args)


def _merge_kernel(h_ref, ua_ref, ob_ref, gab_ref, wc_ref, whg_ref, wo_ref, g_ref, b_ref, o_ref):
    ob = jnp.concatenate([ob_ref[hd] for hd in range(N_HEADS)], axis=-1)
    branch_a = jnp.dot(ua_ref[...], wc_ref[...], preferred_element_type=F32)
    branch_b = jnp.dot(ob, whg_ref[...], preferred_element_type=F32)
    merged = gab_ref[0].astype(F32) * branch_a + gab_ref[1].astype(F32) * branch_b
    mix = jnp.dot(merged.astype(BF16), wo_ref[...], preferred_element_type=F32)
    o_ref[...] = _layer_norm_rows(ALPHA * h_ref[...] + mix, g_ref[...], b_ref[...])


def _merge_call(h, ua, ob, gab, wc, whg, wo, g, b, layer, tm):
    n = h.shape[0]
    row = pl.BlockSpec((tm, D_MODEL), lambda i: (i, 0))
    mat = pl.BlockSpec((None, D_MODEL, D_MODEL), lambda i: (layer, 0, 0))
    vec = pl.BlockSpec((1, D_MODEL), lambda i: (0, 0))
    return pl.pallas_call(
        _merge_kernel, grid=(n // tm,),
        in_specs=[row, row, pl.BlockSpec((N_HEADS, tm, HEAD_DIM), lambda i: (0, i, 0)),
                  pl.BlockSpec((2, tm, D_MODEL), lambda i: (0, i, 0)), mat, mat, mat, vec, vec],
        out_specs=row, out_shape=jax.ShapeDtypeStruct((n, D_MODEL), F32),
        compiler_params=_params(("parallel",), VMEM_LIMIT), name="merge",
    )(h, ua, ob, gab, wc, whg, wo, g.reshape(1, -1), b.reshape(1, -1))


GATE_LANES = 128
CLASS_ROWS = 32


def _router_kernel(h_ref, w_ref, b_ref, cls_ref, rank_ref, gate_ref, count_ref):
    @pl.when(pl.program_id(0) == 0)
    def _():
        count_ref[...] = jnp.zeros_like(count_ref)

    logits = lax.dot_general(w_ref[...], h_ref[...].astype(BF16), (((1,), (1,)), ((), ())),
                             preferred_element_type=F32) + b_ref[...]
    rows = [logits[e:e + 1, :] for e in range(N_EXPERTS)]
    mx = functools.reduce(jnp.maximum, rows)
    ex = [jnp.exp(x - mx) for x in rows]
    den = functools.reduce(lambda a, b: a + b, ex)
    sc = [e / den for e in ex]
    gscore = []
    for grp in range(N_GROUPS):
        s = sc[grp * 4:grp * 4 + 4]
        gscore.append(functools.reduce(jnp.maximum, [s[a] + s[b] for a, b in PAIRS]))
    gsel = jnp.zeros_like(gscore[0], dtype=jnp.int32)
    best = gscore[0]
    for grp in range(1, N_GROUPS):
        better = gscore[grp] > best
        gsel = jnp.where(better, grp, gsel)
        best = jnp.where(better, gscore[grp], best)
    ing = []
    for k in range(EXPERTS_PER_GROUP):
        x = sc[k]
        for grp in range(1, N_GROUPS):
            x = jnp.where(gsel == grp, sc[grp * 4 + k], x)
        ing.append(x)
    i1 = jnp.zeros_like(gsel)
    w1 = ing[0]
    for k in range(1, 4):
        better = ing[k] > w1
        i1 = jnp.where(better, k, i1)
        w1 = jnp.where(better, ing[k], w1)
    i2 = jnp.full_like(gsel, -1)
    w2 = jnp.full_like(w1, -1.0)
    for k in range(4):
        better = (i1 != k) & (ing[k] > w2)
        i2 = jnp.where(better, k, i2)
        w2 = jnp.where(better, ing[k], w2)
    tot = w1 + w2
    g1, g2 = w1 / tot, w2 / tot
    lo, hi = jnp.minimum(i1, i2), jnp.maximum(i1, i2)
    pair = jnp.where(lo == 0, hi - 1, jnp.where(lo == 1, hi + 1, 5))
    cls = gsel * len(PAIRS) + pair
    cls_ref[...] = cls
    tm = cls.shape[1]

    onehot = lax.broadcasted_iota(jnp.int32, (CLASS_ROWS, tm), 0) == cls
    upper = (lax.broadcasted_iota(jnp.int32, (tm, tm), 0) <= lax.broadcasted_iota(jnp.int32, (tm, tm), 1))
    prefix = jnp.dot(jnp.where(onehot, 1.0, 0.0).astype(BF16), jnp.where(upper, 1.0, 0.0).astype(BF16),
                     preferred_element_type=F32)
    seen = count_ref[:, 0:1]
    rank_ref[...] = jnp.sum(jnp.where(onehot, prefix + seen - 1.0, 0.0), axis=0, keepdims=True).astype(jnp.int32)
    count_ref[...] = count_ref[...] + prefix[:, tm - 1:tm]

    first_is_lo = i1 < i2
    gates = (jnp.where(first_is_lo, g1, g2), jnp.where(first_is_lo, g2, g1))
    rid = lax.broadcasted_iota(jnp.int32, (16, tm), 0)
    pieces = jnp.zeros((16, tm), F32)
    rest = list(gates)
    for part in range(3):
        for k in range(2):
            piece = rest[k].astype(BF16).astype(F32)
            rest[k] = rest[k] - piece
            pieces = jnp.where(rid == 2 * part + k, piece, pieces)
    lane = lax.broadcasted_iota(jnp.int32, (16, GATE_LANES), 1)
    row = lax.broadcasted_iota(jnp.int32, (16, GATE_LANES), 0)
    place = jnp.where((row < 6) & (lane == (row & 1)), 1.0, 0.0).astype(BF16)
    gate_ref[...] = _tn(pieces.astype(BF16), place)


def _router_call(h, w_t, b, tm):
    n = h.shape[0]
    return pl.pallas_call(
        _router_kernel, grid=(n // tm,),
        in_specs=[pl.BlockSpec((tm, D_MODEL), lambda i: (i, 0)),
                  pl.BlockSpec((N_EXPERTS, D_MODEL), lambda i: (0, 0)),
                  pl.BlockSpec((N_EXPERTS, 1), lambda i: (0, 0))],
        out_specs=[pl.BlockSpec((1, tm), lambda i: (0, i)), pl.BlockSpec((1, tm), lambda i: (0, i)),
                   pl.BlockSpec((tm, GATE_LANES), lambda i: (i, 0)),
                   pl.BlockSpec((CLASS_ROWS, 128), lambda i: (0, 0))],
        out_shape=[jax.ShapeDtypeStruct((1, n), jnp.int32), jax.ShapeDtypeStruct((1, n), jnp.int32),
                   jax.ShapeDtypeStruct((n, GATE_LANES), F32), jax.ShapeDtypeStruct((CLASS_ROWS, 128), F32)],
        compiler_params=_params(("arbitrary",)), name="router",
    )(h, w_t, b)


ROW_WIDTH = D_MODEL + GATE_LANES


def _scatter_kernel(dest_ref, h_ref, g_ref, init_ref, o_ref, row_ref, sem):
    del init_ref
    tm = h_ref.shape[0]
    base = pl.program_id(0) * tm
    row_ref[:, 0:D_MODEL] = h_ref[...]
    row_ref[:, D_MODEL:ROW_WIDTH] = g_ref[...]

    for r in range(tm):
        pltpu.make_async_copy(row_ref.at[pl.ds(r, 1), :], o_ref.at[pl.ds(dest_ref[base + r], 1), :], sem).start()
    pltpu.make_async_copy(row_ref, o_ref.at[pl.ds(0, tm), :], sem).wait()


def _scatter_call(h, gates, dest, n_out, tm):
    n = h.shape[0]
    return pl.pallas_call(
        _scatter_kernel,
        grid_spec=pltpu.PrefetchScalarGridSpec(
            num_scalar_prefetch=1, grid=(n // tm,),
            in_specs=[pl.BlockSpec((tm, D_MODEL), lambda i, *_: (i, 0)),
                      pl.BlockSpec((tm, GATE_LANES), lambda i, *_: (i, 0)),
                      pl.BlockSpec(memory_space=pl.ANY)],
            out_specs=pl.BlockSpec(memory_space=pl.ANY),
            scratch_shapes=[pltpu.VMEM((tm, ROW_WIDTH), F32), pltpu.SemaphoreType.DMA(())],
        ),
        out_shape=jax.ShapeDtypeStruct((n_out, ROW_WIDTH), F32),
        input_output_aliases={3: 0},
        compiler_params=_params(("arbitrary",)), name="scatter",
    )(dest, h, gates, jnp.zeros((n_out, ROW_WIDTH), F32))


def _gather_ln_kernel(dest_ref, h_ref, y_ref, g_ref, b_ref, *rest, split):
    *o_refs, row_ref, sem = rest
    tm = h_ref.shape[0]
    i = pl.program_id(0)
    base = i * tm

    for r in range(tm):
        pltpu.make_async_copy(y_ref.at[pl.ds(dest_ref[base + r], 1), :], row_ref.at[pl.ds(r, 1), :], sem).start()
    pltpu.make_async_copy(y_ref.at[pl.ds(0, tm), :], row_ref, sem).wait()
    out = _layer_norm_rows(ALPHA * h_ref[...] + row_ref[...], g_ref[...], b_ref[...])
    if split is None:
        o_refs[0][...] = out
    else:
        @pl.when(i < split)
        def _():
            o_refs[0][...] = out

        @pl.when(i >= split)
        def _():
            o_refs[1][...] = out


def _gather_ln_call(h, ys, dest, g, b, tm, split_rows=None):
    n = h.shape[0]
    row = pl.BlockSpec((tm, D_MODEL), lambda i, *_: (i, 0))
    vec = pl.BlockSpec((1, D_MODEL), lambda i, *_: (0, 0))
    if split_rows is None:
        split, out_specs, out_shape = None, row, jax.ShapeDtypeStruct((n, D_MODEL), F32)
    else:
        split = split_rows // tm
        out_specs = [pl.BlockSpec((tm, D_MODEL), lambda i, *_: (jnp.minimum(i, split - 1), 0)),
                     pl.BlockSpec((tm, D_MODEL), lambda i, *_: (jnp.maximum(i - split, 0), 0))]
        out_shape = [jax.ShapeDtypeStruct((split_rows, D_MODEL), F32),
                     jax.ShapeDtypeStruct((n - split_rows, D_MODEL), F32)]
    return pl.pallas_call(
        functools.partial(_gather_ln_kernel, split=split),
        grid_spec=pltpu.PrefetchScalarGridSpec(
            num_scalar_prefetch=1, grid=(n // tm,),
            in_specs=[row, pl.BlockSpec(memory_space=pl.ANY), vec, vec],
            out_specs=out_specs,
            scratch_shapes=[pltpu.VMEM((tm, D_MODEL), F32), pltpu.SemaphoreType.DMA(())],
        ),
        out_shape=out_shape,
        compiler_params=_params(("arbitrary",)), name="gather_ln",
    )(dest, h, ys, g.reshape(1, -1), b.reshape(1, -1))


def _gelu_tanh(x):
    return 0.5 * x * (1.0 + jnp.tanh(0.7978845608028654 * (x + 0.044715 * x * x * x)))


def _moe_kernel(ea_ref, eb_ref, used_ref, x_ref, w1a_ref, w2a_ref, w1b_ref, w2b_ref, o_ref):
    @pl.when(pl.program_id(0) < used_ref[0])
    def _():
        x = x_ref[:, 0:D_MODEL].astype(BF16)

        def expert(w1_ref, w2_ref):
            hid = _gelu_tanh(jnp.dot(x, w1_ref[...], preferred_element_type=F32))
            return jnp.dot(hid.astype(BF16), w2_ref[...], preferred_element_type=F32)

        g = x_ref[:, D_MODEL:ROW_WIDTH]
        o_ref[...] = g[:, 0:1] * expert(w1a_ref, w2a_ref) + g[:, 1:2] * expert(w1b_ref, w2b_ref)

    @pl.when(pl.program_id(0) >= used_ref[0])
    def _():
        o_ref[...] = jnp.zeros_like(o_ref)


def _moe_call(xs, blk_ea, blk_eb, n_used, w1, w2, layer, tm):
    npad = xs.shape[0]
    w1a = pl.BlockSpec((None, None, D_MODEL, D_EXPERT), lambda i, ea, eb, u: (layer, ea[i], 0, 0))
    w2a = pl.BlockSpec((None, None, D_EXPERT, D_MODEL), lambda i, ea, eb, u: (layer, ea[i], 0, 0))
    w1b = pl.BlockSpec((None, None, D_MODEL, D_EXPERT), lambda i, ea, eb, u: (layer, eb[i], 0, 0))
    w2b = pl.BlockSpec((None, None, D_EXPERT, D_MODEL), lambda i, ea, eb, u: (layer, eb[i], 0, 0))
    return pl.pallas_call(
        _moe_kernel,
        grid_spec=pltpu.PrefetchScalarGridSpec(
            num_scalar_prefetch=3, grid=(npad // tm,),
            in_specs=[pl.BlockSpec((tm, ROW_WIDTH), lambda i, *_: (i, 0)), w1a, w2a, w1b, w2b],
            out_specs=pl.BlockSpec((tm, D_MODEL), lambda i, *_: (i, 0)),
        ),
        out_shape=jax.ShapeDtypeStruct((npad, D_MODEL), F32),
        compiler_params=_params(("arbitrary",), VMEM_LIMIT), name="moe",
    )(blk_ea, blk_eb, n_used, xs, w1, w2, w1, w2)


def _routing_tables(cls, rank, counts, tm):
    nblk = cls.shape[0] // tm + N_CLASSES
    padded = (counts + tm - 1) // tm * tm
    pad_end = jnp.cumsum(padded)
    classes = jnp.arange(N_CLASSES, dtype=jnp.int32)
    dest = rank + jnp.sum(jnp.where(cls[:, None] == classes, pad_end - padded, 0), axis=1)
    n_used = pad_end[-1] // tm
    first_row = jnp.minimum(jnp.arange(nblk, dtype=jnp.int32), n_used - 1) * tm
    blk_cls = jnp.minimum(jnp.sum(first_row[:, None] >= pad_end, axis=1).astype(jnp.int32), N_CLASSES - 1)
    grp, pr = blk_cls // len(PAIRS), blk_cls % len(PAIRS)
    lo = (pr >= 3).astype(jnp.int32) + (pr >= 5).astype(jnp.int32)
    hi = jnp.where(pr < 3, pr + 1, jnp.where(pr < 5, pr - 1, 3))
    blk_ea = grp * EXPERTS_PER_GROUP + lo
    blk_eb = grp * EXPERTS_PER_GROUP + hi
    return dest, nblk * tm, blk_ea, blk_eb, n_used.reshape(1).astype(jnp.int32)


def _block_flags(seq_lens, t):
    first, last = [], []
    for length in seq_lens:
        nb = length // t
        first += [1] + [0] * (nb - 1)
        last += [0] * (nb - 1) + [1]
    return jnp.asarray(np.array(first, np.int32)), jnp.asarray(np.array(last, np.int32))


def _trunk(xa, xb, seq_lens, p, *, tm, tm_proj, t_conv, t_gla, tm_moe):
    (ln_in_g, ln_in_b, w_in, dw_w, dw_b, conv_ln_g, conv_ln_b, w_conv_out, hg_lower, hg_norm_g, w_hg_out,
     w_o, ln1_g, ln1_b, w_router, b_router, w_e1, w_e2, ln2_g, ln2_b) = p
    depth = w_in.shape[0]
    conv_first, conv_last = _block_flags(seq_lens, t_conv)
    gla_first, gla_last = _block_flags(seq_lens, t_gla)

    lb_p = jax.nn.softmax(hg_lower.astype(F32), axis=0)
    lb_all = jnp.clip(jnp.cumsum(lb_p, axis=0) - lb_p[0:1], 0.0, 1.0)
    w_in_b = w_in.reshape(depth, D_MODEL, 9, D_MODEL).transpose(0, 2, 1, 3).astype(BF16)
    w_router_t = w_router.T.astype(BF16)
    b_router_c = b_router.reshape(N_EXPERTS, 1).astype(F32)
    wc_b, whg_b, wo_b = w_conv_out.astype(BF16), w_hg_out.astype(BF16), w_o.astype(BF16)
    w_e1_b, w_e2_b = w_e1.astype(BF16), w_e2.astype(BF16)

    h = _ln_call(xa, xb, ln_in_g, ln_in_b, tm)
    for l in range(depth):
        u, qvo, f, gab = _proj_call(h, w_in_b, lb_all[l].reshape(2, 1, D_MODEL), l, tm_proj)
        ua = _conv_call(u, conv_first, conv_last, dw_w[l], dw_b[l], conv_ln_g[l], conv_ln_b[l], t_conv)
        o_bw = _gla_call(gla_last, qvo, f, 1, t_gla)
        ob = _gla_call(gla_first, qvo, f, 0, t_gla, obw=o_bw, g=hg_norm_g[l].astype(F32))
        h1 = _merge_call(h, ua, ob, gab, wc_b, whg_b, wo_b, ln1_g[l], ln1_b[l], l, tm)
        cls, rank, gates, counts = _router_call(h1, w_router_t, b_router_c, tm)
        dest, n_rows, blk_ea, blk_eb, n_used = _routing_tables(
            cls[0], rank[0], counts[:N_CLASSES, 0].astype(jnp.int32), tm_moe)
        xs = _scatter_call(h1, gates, dest, n_rows, tm)
        ys = _moe_call(xs, blk_ea, blk_eb, n_used, w_e1_b, w_e2_b, l, tm_moe)
        h = _gather_ln_call(h1, ys, dest, ln2_g[l], ln2_b[l], tm,
                            split_rows=xa.shape[0] if l == depth - 1 else None)
    return h


def kernel(x_prompt, x_sample, ln_in_g, ln_in_b, w_in, dw_w, dw_b, conv_ln_g, conv_ln_b, w_conv_out, hg_lower,
           hg_norm_g, w_hg_out, w_o, ln1_g, ln1_b, w_router, b_router, w_e1, w_e2, ln2_g, ln2_b):
    bp, lp, d = x_prompt.shape
    bs, ls, _ = x_sample.shape
    seq_lens = (lp,) * bp + (ls,) * bs
    p = (ln_in_g, ln_in_b, w_in, dw_w, dw_b, conv_ln_g, conv_ln_b, w_conv_out, hg_lower, hg_norm_g, w_hg_out,
         w_o, ln1_g, ln1_b, w_router, b_router, w_e1, w_e2, ln2_g, ln2_b)
    yp, ys = _trunk(x_prompt.reshape(bp * lp, d), x_sample.reshape(bs * ls, d), seq_lens, p,
                    tm=512, tm_proj=1024, t_conv=1024, t_gla=2048, tm_moe=256)
    return (yp.reshape(bp, lp, d), ys.reshape(bs, ls, d))
```
